```python
import math
import jax
import jax.numpy as jnp
from jax import lax
import numpy as np

D_MODEL = 1024
BATCH = 2
SEQ = 8192
DEPTH = 1
DEC_BATCH = 128
DEC_SEQ = 8
PAST_LEN = 8192
PAGE_SIZE = 128

RET_HEADS = 4
RET_QK_DIM = 128
RET_V_DIM = 256
RET_QK_WIDTH = RET_HEADS * RET_QK_DIM
RET_V_WIDTH = RET_HEADS * RET_V_DIM
RET_CHUNK = 128
ROPE_BASE = 10000.0
ATT_HEADS = 8
ATT_HEAD_DIM = 64
ATT_WIDTH = ATT_HEADS * ATT_HEAD_DIM
IDX_HEADS = 8
IDX_DIM = 64
TOPK_MAX = 256
Q_BLOCK = 128
REL_BUCKETS = 32
REL_MAX_EXACT = REL_BUCKETS // 2
REL_MAX_DIST = 128
FFN_HIDDEN = -(-8 * D_MODEL // (3 * 256)) * 256
NORM_EPS = 1e-6
LN_EPS = 1e-5
IN_SPLITS = (RET_QK_WIDTH, RET_QK_WIDTH, RET_V_WIDTH, RET_V_WIDTH,
             ATT_WIDTH, ATT_WIDTH, ATT_WIDTH,
             IDX_HEADS * IDX_DIM, IDX_DIM, IDX_HEADS,
             D_MODEL, D_MODEL)
IN_WIDTH = sum(IN_SPLITS)
SPLIT_POINTS = tuple(sum(IN_SPLITS[:i + 1]) for i in range(len(IN_SPLITS) - 1))

kernel_name = 'retention_dsa_hybrid_step'

F32 = jnp.float32


def rmsnorm(x, g):
    xf = x.astype(F32)
    y = xf * lax.rsqrt(jnp.mean(xf * xf, axis=-1, keepdims=True) + NORM_EPS)
    return (y * g.astype(F32)).astype(x.dtype)


def layernorm(x, g, b):
    xf = x.astype(F32)
    mu = jnp.mean(xf, axis=-1, keepdims=True)
    var = jnp.mean(jnp.square(xf - mu), axis=-1, keepdims=True)
    y = (xf - mu) * lax.rsqrt(var + LN_EPS)
    return (y * g.astype(F32) + b.astype(F32)).astype(x.dtype)


def rotate(x, pos):
    d = x.shape[-1]
    inv_freq = ROPE_BASE ** (-jnp.arange(0, d, 2, dtype=F32) / d)
    ang = pos.astype(F32)[:, None] * inv_freq[None, :]
    cos = jnp.cos(ang)[None, :, None, :]
    sin = jnp.sin(ang)[None, :, None, :]
    xf = x.astype(F32)
    x1, x2 = xf[..., : d // 2], xf[..., d // 2:]
    return jnp.concatenate([x1 * cos - x2 * sin, x1 * sin + x2 * cos], axis=-1).astype(x.dtype)


def project(h, w_in, idx_ln_g, idx_ln_b):
    B, T, _ = h.shape
    (q_r, k_r, v_r, g_r, q_a, k_a, v_a, q_i, k_i, w_i, gate_r, gate_a) = jnp.split(h @ w_in, SPLIT_POINTS, axis=-1)
    return (q_r.reshape(B, T, RET_HEADS, RET_QK_DIM),
            k_r.reshape(B, T, RET_HEADS, RET_QK_DIM),
            v_r.reshape(B, T, RET_HEADS, RET_V_DIM),
            g_r,
            q_a.reshape(B, T, ATT_HEADS, ATT_HEAD_DIM),
            k_a.reshape(B, T, ATT_HEADS, ATT_HEAD_DIM),
            v_a.reshape(B, T, ATT_HEADS, ATT_HEAD_DIM),
            q_i.reshape(B, T, IDX_HEADS, IDX_DIM) * (IDX_DIM ** -0.5),
            layernorm(k_i, idx_ln_g, idx_ln_b),
            w_i * (IDX_HEADS ** -0.5),
            gate_r, gate_a)


def retention_scan(q, k, v, s0):
    B, T, H, _ = q.shape
    dv = v.shape[-1]
    C = RET_CHUNK if T % RET_CHUNK == 0 else T
    N = T // C
    log_g = jnp.log1p(-jnp.exp2(-5.0 - jnp.arange(H, dtype=F32)))
    n = jnp.arange(C, dtype=F32)
    diff = n[:, None] - n[None, :]
    dmask = jnp.where(diff >= 0, jnp.exp(log_g[:, None, None] * jnp.maximum(diff, 0.0)), 0.0)
    q_dec = jnp.exp(log_g[:, None] * (n + 1.0))[..., None]
    k_dec = jnp.exp(log_g[:, None] * (C - 1.0 - n))[..., None]
    c_dec = jnp.exp(log_g * C)[:, None, None]

    def to_chunks(a):
        return a.astype(F32).reshape(B, N, C, H, a.shape[-1]).transpose(1, 0, 3, 2, 4)

    def step(S, inp):
        qc, kc, vc = inp
        scores = jnp.einsum('bhnd,bhmd->bhnm', qc, kc) * dmask
        o = (jnp.einsum('bhnm,bhme->bhne', scores, vc)
             + jnp.einsum('bhnd,bhde->bhne', qc * q_dec, S))
        S = S * c_dec + jnp.einsum('bhmd,bhme->bhde', kc * k_dec, vc)
        return S, o

    S, o = lax.scan(step, s0.astype(F32), (to_chunks(q), to_chunks(k), to_chunks(v)))
    o = o.transpose(1, 0, 3, 2, 4).reshape(B, T, H, dv)
    return o, S


def retention_branch(q, k, v, g, pos, s0):
    q = rotate(q, pos)
    k = rotate(k, pos) * (RET_QK_DIM ** -0.5)
    o, S = retention_scan(q, k, v, s0)
    mu = jnp.mean(o, axis=-1, keepdims=True)
    var = jnp.mean(jnp.square(o - mu), axis=-1, keepdims=True)
    on = (o - mu) * lax.rsqrt(var + LN_EPS)
    B, T = g.shape[:2]
    y = jax.nn.silu(g) * on.reshape(B, T, RET_V_WIDTH).astype(g.dtype)
    return y, S


def indexer_scores(q_i, w_i, k_i):
    s = jax.nn.relu(jnp.einsum('bqhd,bsd->bqhs', q_i, k_i).astype(F32))
    return jnp.einsum('bqhs,bqh->bqs', s, w_i.astype(F32))


def select_keys(scores, q_pos, n_keep):
    L = scores.shape[-1]
    visible = jnp.arange(L, dtype=jnp.int32)[None, None, :] <= q_pos[None, :, None]
    _, idx = lax.top_k(jnp.where(visible, scores, -jnp.inf), n_keep)
    return idx.astype(jnp.int32)


def rel_bucket(dist):
    small = dist < REL_MAX_EXACT
    nf = jnp.maximum(dist, 1).astype(F32)
    large = REL_MAX_EXACT + (jnp.log(nf / REL_MAX_EXACT) / math.log(REL_MAX_DIST / REL_MAX_EXACT)
                             * (REL_BUCKETS - REL_MAX_EXACT)).astype(jnp.int32)
    large = jnp.minimum(large, REL_BUCKETS - 1)
    return jnp.where(small, dist, large)


def sparse_attend(q, k_sel, v_sel, q_pos, sel_pos, rel_bias):
    dist = q_pos[None, :, None] - sel_pos
    valid = dist >= 0
    bias = rel_bias[rel_bucket(jnp.maximum(dist, 0))].astype(F32)
    logits = jnp.einsum('bqhd,bqkhd->bqkh', q, k_sel).astype(F32) * (ATT_HEAD_DIM ** -0.5) + bias
    logits = jnp.where(valid[..., None], logits, -jnp.inf)
    p = jax.nn.softmax(logits, axis=2)
    o = jnp.einsum('bqkh,bqkhd->bqhd', p.astype(v_sel.dtype), v_sel)
    return o.reshape(q.shape[0], q.shape[1], ATT_WIDTH)


def gather_rows(rows, idx):
    return jax.vmap(lambda r, i: r[i])(rows, idx)


def prompt_sparse_attention(q_a, k_a, v_a, q_i, w_i, k_i, rel_bias):
    B, T = q_a.shape[:2]
    qb = Q_BLOCK if T % Q_BLOCK == 0 else T
    nb = T // qb
    n_keep = min(TOPK_MAX, max(T // 4, 1))

    def blocks(a):
        return a.reshape((B, nb, qb) + a.shape[2:]).swapaxes(0, 1)

    starts = jnp.arange(nb, dtype=jnp.int32) * qb

    def one_block(args):
        qa, qi, wi, start = args
        q_pos = start + jnp.arange(qb, dtype=jnp.int32)
        idx = select_keys(indexer_scores(qi, wi, k_i), q_pos, n_keep)
        return sparse_attend(qa, gather_rows(k_a, idx), gather_rows(v_a, idx), q_pos, idx, rel_bias)

    out = lax.map(one_block, (blocks(q_a), blocks(q_i), blocks(w_i), starts))
    return out.swapaxes(0, 1).reshape(B, T, ATT_WIDTH)


def sample_sparse_attention(q_a, k_a, v_a, q_i, w_i, k_i, cache_k, cache_v, cache_idx_k, page_table, rel_bias):
    DB, T = q_a.shape[:2]
    P = page_table.shape[1] * PAGE_SIZE
    L = P + T
    past_idx_k = cache_idx_k[page_table].reshape(DB, P, IDX_DIM).astype(k_i.dtype)
    keys_i = jnp.concatenate([past_idx_k, k_i], axis=1)
    q_pos = P + jnp.arange(T, dtype=jnp.int32)
    idx = select_keys(indexer_scores(q_i, w_i, keys_i), q_pos, min(TOPK_MAX, L // 4))
    is_past = (idx < P)[..., None, None]
    s_past = jnp.minimum(idx, P - 1)
    phys = jnp.take_along_axis(page_table, (s_past // PAGE_SIZE).reshape(DB, -1), axis=1).reshape(idx.shape)
    off = s_past % PAGE_SIZE
    s_new = jnp.clip(idx - P, 0, T - 1)
    k_sel = jnp.where(is_past, cache_k[phys, off].astype(k_a.dtype), gather_rows(k_a, s_new))
    v_sel = jnp.where(is_past, cache_v[phys, off].astype(v_a.dtype), gather_rows(v_a, s_new))
    return sparse_attend(q_a, k_sel, v_sel, q_pos, idx, rel_bias)


def branch_merge_ffn(x, y_ret, y_att, gate_r, gate_a, w_ret_branch, w_att_branch, w_out,
                     norm2_g, w_ffn_in, w_ffn_out):
    m = (jax.nn.sigmoid(gate_r) * (y_ret @ w_ret_branch)
         + jax.nn.sigmoid(gate_a) * (y_att @ w_att_branch))
    x = x + m @ w_out
    u, gt = jnp.split(rmsnorm(x, norm2_g) @ w_ffn_in, 2, axis=-1)
    return x + (jax.nn.silu(gt) * u) @ w_ffn_out


def setup_inputs(seed: int = 0) -> dict:
    key = jax.random.key(seed)
    ks = jax.random.split(key, 20)
    n_pages = PAST_LEN // PAGE_SIZE
    n_pool = (DEC_BATCH * n_pages * 5) // 4
    nrm = jax.random.normal
    page_table = jax.random.permutation(ks[0], n_pool)[: DEC_BATCH * n_pages].reshape(DEC_BATCH, n_pages).astype(jnp.int32)
    return {
        'x_prompt': nrm(ks[1], (BATCH, SEQ, D_MODEL), F32),
        'x_sample': nrm(ks[2], (DEC_BATCH, DEC_SEQ, D_MODEL), F32),
        'cache_k': nrm(ks[3], (DEPTH, n_pool, PAGE_SIZE, ATT_HEADS, ATT_HEAD_DIM), F32),
        'cache_v': nrm(ks[4], (DEPTH, n_pool, PAGE_SIZE, ATT_HEADS, ATT_HEAD_DIM), F32),
        'cache_idx_k': nrm(ks[5], (DEPTH, n_pool, PAGE_SIZE, IDX_DIM), F32),
        'state_ret': 0.5 * nrm(ks[6], (DEPTH, DEC_BATCH, RET_HEADS, RET_QK_DIM, RET_V_DIM), F32),
        'page_table': page_table,
        'norm1_g': 1.0 + 0.02 * nrm(ks[7], (DEPTH, D_MODEL), F32),
        'w_in': nrm(ks[8], (DEPTH, D_MODEL, IN_WIDTH), F32) * D_MODEL ** -0.5,
        'idx_ln_g': 1.0 + 0.02 * nrm(ks[9], (DEPTH, IDX_DIM), F32),
        'idx_ln_b': 0.02 * nrm(ks[10], (DEPTH, IDX_DIM), F32),
        'w_ret_branch': nrm(ks[11], (DEPTH, RET_V_WIDTH, D_MODEL), F32) * RET_V_WIDTH ** -0.5,
        'w_att_branch': nrm(ks[12], (DEPTH, ATT_WIDTH, D_MODEL), F32) * ATT_WIDTH ** -0.5,
        'w_out': nrm(ks[13], (DEPTH, D_MODEL, D_MODEL), F32) * D_MODEL ** -0.5,
        'norm2_g': 1.0 + 0.02 * nrm(ks[14], (DEPTH, D_MODEL), F32),
        'w_ffn_in': nrm(ks[15], (DEPTH, D_MODEL, 2 * FFN_HIDDEN), F32) * D_MODEL ** -0.5,
        'w_ffn_out': nrm(ks[16], (DEPTH, FFN_HIDDEN, D_MODEL), F32) * FFN_HIDDEN ** -0.5,
        'rel_bias': 0.5 * nrm(ks[17], (REL_BUCKETS, ATT_HEADS), F32),
        'norm_f_g': 1.0 + 0.02 * nrm(ks[18], (D_MODEL,), F32),
    }


def reference(x_prompt, x_sample, cache_k, cache_v, cache_idx_k, state_ret, page_table,
              norm1_g, w_in, idx_ln_g, idx_ln_b, w_ret_branch, w_att_branch, w_out,
              norm2_g, w_ffn_in, w_ffn_out, rel_bias, norm_f_g):
    B = x_prompt.shape[0]
    pos_p = jnp.arange(x_prompt.shape[1], dtype=jnp.int32)
    pos_s = page_table.shape[1] * PAGE_SIZE + jnp.arange(x_sample.shape[1], dtype=jnp.int32)
    xp, xs = x_prompt, x_sample
    kp_l, vp_l, ip_l, sp_l = [], [], [], []
    ks_l, vs_l, is_l, ss_l = [], [], [], []
    for l in range(DEPTH):
        (q_r, k_r, v_r, g_r, q_a, k_a, v_a, q_i, k_i, w_i, gate_r, gate_a) = project(
            rmsnorm(xp, norm1_g[l]), w_in[l], idx_ln_g[l], idx_ln_b[l])
        s0 = jnp.zeros((B, RET_HEADS, RET_QK_DIM, RET_V_DIM), F32)
        y_ret, S = retention_branch(q_r, k_r, v_r, g_r, pos_p, s0)
        y_att = prompt_sparse_attention(q_a, k_a, v_a, q_i, w_i, k_i, rel_bias)
        xp = branch_merge_ffn(xp, y_ret, y_att, gate_r, gate_a, w_ret_branch[l], w_att_branch[l],
                              w_out[l], norm2_g[l], w_ffn_in[l], w_ffn_out[l])
        kp_l.append(k_a)
        vp_l.append(v_a)
        ip_l.append(k_i)
        sp_l.append(S.astype(x_prompt.dtype))
        (q_r, k_r, v_r, g_r, q_a, k_a, v_a, q_i, k_i, w_i, gate_r, gate_a) = project(
            rmsnorm(xs, norm1_g[l]), w_in[l], idx_ln_g[l], idx_ln_b[l])
        y_ret, S = retention_branch(q_r, k_r, v_r, g_r, pos_s, state_ret[l])
        y_att = sample_sparse_attention(q_a, k_a, v_a, q_i, w_i, k_i, cache_k[l], cache_v[l],
                                        cache_idx_k[l], page_table, rel_bias)
        xs = branch_merge_ffn(xs, y_ret, y_att, gate_r, gate_a, w_ret_branch[l], w_att_branch[l],
                              w_out[l], norm2_g[l], w_ffn_in[l], w_ffn_out[l])
        ks_l.append(k_a)
        vs_l.append(v_a)
        is_l.append(k_i)
        ss_l.append(S.astype(state_ret.dtype))
    y_prompt = rmsnorm(xp, norm_f_g)
    y_sample = rmsnorm(xs, norm_f_g)
    return (y_prompt, y_sample,
            jnp.stack(kp_l), jnp.stack(vp_l), jnp.stack(ip_l), jnp.stack(sp_l),
            jnp.stack(ks_l), jnp.stack(vs_l), jnp.stack(is_l), jnp.stack(ss_l))
```

```python
import functools
import math

import numpy as np
import jax
import jax.numpy as jnp
from jax import lax
from jax.experimental import pallas as pl
from jax.experimental.pallas import tpu as pltpu

F32 = jnp.float32
BF16 = jnp.bfloat16

D_MODEL = 1024
PAGE_SIZE = 128
RET_HEADS = 4
RET_QK_DIM = 128
RET_V_DIM = 256
RET_CHUNK = 128
ROPE_BASE = 10000.0
ATT_HEADS = 8
ATT_HEAD_DIM = 64
ATT_WIDTH = ATT_HEADS * ATT_HEAD_DIM
IDX_HEADS = 8
IDX_DIM = 64
TOPK_MAX = 256
REL_BUCKETS = 32
REL_MAX_EXACT = 16
REL_MAX_DIST = 128
REL_SATURATE = 128
REL_FIRST_SATURATED = 113
FFN_HIDDEN = 2816
NORM_EPS = 1e-6
LN_EPS = 1e-5

LANES = 128
VMEM_LIMIT_BYTES = 56 * 1024 * 1024

NEG_BIG = -1e30
F32_MAX = float(np.finfo(np.float32).max)

NT_DIMS = (((1,), (1,)), ((), ()))


def _resident(shape, index_map):
    return pl.BlockSpec(shape, index_map, pipeline_mode=pl.Buffered(1))


_C_QR, _C_KR, _C_VR, _C_GR = 0, 512, 1024, 2048
_C_QA, _C_KA, _C_VA, _C_QI = 3072, 3584, 4096, 4608
_C_SGR, _C_SGA, _C_END = 5120, 6144, 7168


def _proj_kernel(x_ref, g_ref, wm_ref, wk_ref, ww_ref, cos_ref, sin_ref, lng_ref, lnb_ref,
                 qr_ref, kr_ref, vr_ref, gr_ref, qa_ref, ka_ref, va_ref, kab_ref, vab_ref,
                 qi_ref, ki_ref, kk_ref, wi_ref, sgr_ref, sga_ref):
    x = x_ref[...]
    h = (x * lax.rsqrt(jnp.mean(x * x, axis=-1, keepdims=True) + NORM_EPS) * g_ref[...]).astype(BF16)

    def mm(a, b):
        return jnp.dot(h, wm_ref[:, a:b], preferred_element_type=F32)

    cos = cos_ref[...]
    sin = sin_ref[...]
    q = mm(_C_QR, _C_KR)
    k = mm(_C_KR, _C_VR)
    for hd in range(RET_HEADS):
        sl = slice(hd * RET_QK_DIM, (hd + 1) * RET_QK_DIM)
        qh = q[:, sl]
        kh = k[:, sl]
        qr_ref[:, sl] = (qh * cos + pltpu.roll(qh, RET_QK_DIM // 2, 1) * sin).astype(qr_ref.dtype)
        kr_ref[:, sl] = ((kh * cos + pltpu.roll(kh, RET_QK_DIM // 2, 1) * sin)
                         * (RET_QK_DIM ** -0.5)).astype(kr_ref.dtype)
    vr_ref[...] = mm(_C_VR, _C_GR).astype(vr_ref.dtype)
    gr_ref[...] = mm(_C_GR, _C_QA)
    qa_ref[...] = (mm(_C_QA, _C_KA) * (ATT_HEAD_DIM ** -0.5)).astype(BF16)
    ka = mm(_C_KA, _C_VA)
    ka_ref[...] = ka
    kab_ref[...] = ka.astype(BF16)
    va = mm(_C_VA, _C_QI)
    va_ref[...] = va
    vab_ref[...] = va.astype(BF16)
    qi_ref[...] = (mm(_C_QI, _C_SGR) * (IDX_DIM ** -0.5)).astype(BF16)
    sgr_ref[...] = jax.nn.sigmoid(mm(_C_SGR, _C_SGA))
    sga_ref[...] = jax.nn.sigmoid(mm(_C_SGA, _C_END))

    kd = jnp.dot(h, wk_ref[...], preferred_element_type=F32)
    mu = jnp.mean(kd, axis=-1, keepdims=True)
    var = jnp.mean(jnp.square(kd - mu), axis=-1, keepdims=True)
    kn = (kd - mu) * lax.rsqrt(var + LN_EPS) * lng_ref[...] + lnb_ref[...]
    ki_ref[...] = kn[:, :IDX_DIM]
    kk_ref[...] = kn.astype(BF16)
    wi = jnp.dot(h, ww_ref[...], preferred_element_type=F32)
    wi_ref[...] = wi[:, :IDX_HEADS] * (IDX_HEADS ** -0.5)


def _proj(x, g, wm, wk, ww, cos_t, sin_t, lng, lnb, *, tm, act_dtype):
    rows = x.shape[0]
    nrot = cos_t.shape[0] // tm
    grid = (rows // tm,)

    def row(w):
        return pl.BlockSpec((tm, w), lambda i: (i, 0))

    def full(a):
        return _resident(a.shape, lambda i: (0,) * a.ndim)

    rot = pl.BlockSpec((tm, LANES), lambda i: (i % nrot, 0))
    widths = [(512, act_dtype), (512, act_dtype), (1024, act_dtype), (1024, F32),
              (512, BF16), (512, F32), (512, F32), (512, BF16), (512, BF16),
              (512, BF16), (IDX_DIM, F32), (LANES, BF16), (IDX_HEADS, F32),
              (1024, F32), (1024, F32)]
    return pl.pallas_call(
        _proj_kernel,
        grid=grid,
        in_specs=[row(D_MODEL), full(g), full(wm), full(wk), full(ww), rot, rot, full(lng), full(lnb)],
        out_specs=[row(w) for w, _ in widths],
        out_shape=[jax.ShapeDtypeStruct((rows, w), dt) for w, dt in widths],
        compiler_params=pltpu.CompilerParams(dimension_semantics=("parallel",),
                                             vmem_limit_bytes=VMEM_LIMIT_BYTES),
        name="proj",
    )(x, g, wm, wk, ww, cos_t, sin_t, lng, lnb)


def _ret_kernel(q_ref, k_ref, v_ref, g_ref, s0_ref, dm_ref, qd_ref, kd_ref, cd_ref,
                y_ref, s_ref, *, mxu_dtype):
    @pl.when(pl.program_id(1) == 0)
    def _():
        s_ref[...] = s0_ref[...]

    for h in range(RET_HEADS):
        qk = slice(h * RET_QK_DIM, (h + 1) * RET_QK_DIM)
        vv = slice(h * RET_V_DIM, (h + 1) * RET_V_DIM)
        q = q_ref[:, qk].astype(F32)
        k = k_ref[:, qk].astype(F32)
        v = v_ref[:, vv].astype(mxu_dtype)
        state = s_ref[h]
        sc = lax.dot_general(q.astype(mxu_dtype), k.astype(mxu_dtype), NT_DIMS,
                             preferred_element_type=F32) * dm_ref[h]
        o = (jnp.dot(sc.astype(mxu_dtype), v, preferred_element_type=F32)
             + jnp.dot((q * qd_ref[h]).astype(mxu_dtype), state.astype(mxu_dtype),
                       preferred_element_type=F32))
        kdec = (k * kd_ref[h]).T.astype(mxu_dtype)
        s_ref[h] = state * cd_ref[h] + jnp.dot(kdec, v, preferred_element_type=F32)
        mu = jnp.mean(o, axis=-1, keepdims=True)
        var = jnp.mean(jnp.square(o - mu), axis=-1, keepdims=True)
        on = (o - mu) * lax.rsqrt(var + LN_EPS)
        y_ref[:, vv] = (jax.nn.silu(g_ref[:, vv]) * on).astype(y_ref.dtype)


def _retention(q, k, v, g, s0, chunk):
    b, t, _ = q.shape
    n = t // chunk
    log_g = np.log1p(-np.exp2(-5.0 - np.arange(RET_HEADS, dtype=np.float64)))
    pos = np.arange(chunk, dtype=np.float64)
    diff = pos[:, None] - pos[None, :]
    dmask = np.where(diff >= 0, np.exp(log_g[:, None, None] * np.maximum(diff, 0.0)), 0.0)
    q_dec = np.broadcast_to(np.exp(log_g[:, None] * (pos + 1.0))[..., None], (RET_HEADS, chunk, RET_QK_DIM))
    k_dec = np.broadcast_to(np.exp(log_g[:, None] * (chunk - 1.0 - pos))[..., None], (RET_HEADS, chunk, RET_QK_DIM))
    c_dec = np.broadcast_to(np.exp(log_g * chunk)[:, None, None], (RET_HEADS, 1, RET_V_DIM))
    consts = [jnp.asarray(a, F32) for a in (dmask, q_dec, k_dec, c_dec)]
    mxu_dtype = BF16 if chunk % 16 == 0 else F32

    def tok(w):
        return pl.BlockSpec((None, chunk, w), lambda i, j: (i, j, 0))

    def const(a):
        return _resident(a.shape, lambda i, j: (0,) * a.ndim)

    state = pl.BlockSpec((None, RET_HEADS, RET_QK_DIM, RET_V_DIM), lambda i, j: (i, 0, 0, 0))
    return pl.pallas_call(
        functools.partial(_ret_kernel, mxu_dtype=mxu_dtype),
        grid=(b, n),
        in_specs=[tok(512), tok(512), tok(1024), tok(1024), state] + [const(a) for a in consts],
        out_specs=[tok(1024), state],
        out_shape=[jax.ShapeDtypeStruct((b, t, 1024), BF16),
                   jax.ShapeDtypeStruct((b, RET_HEADS, RET_QK_DIM, RET_V_DIM), F32)],
        compiler_params=pltpu.CompilerParams(dimension_semantics=("parallel", "arbitrary"),
                                             vmem_limit_bytes=VMEM_LIMIT_BYTES),
        name="retention",
    )(q, k, v, g, s0, *consts)


def _rel_bucket_np(dist):
    dist = np.asarray(dist)
    nf = np.maximum(dist, 1).astype(np.float64)
    large = REL_MAX_EXACT + (np.log(nf / REL_MAX_EXACT) / math.log(REL_MAX_DIST / REL_MAX_EXACT)
                             * (REL_BUCKETS - REL_MAX_EXACT)).astype(np.int32)
    return np.where(dist < REL_MAX_EXACT, dist, np.minimum(large, REL_BUCKETS - 1))


def _fold(op, a):
    parts = [a[:, c * LANES:(c + 1) * LANES] for c in range(a.shape[1] // LANES)]
    return functools.reduce(op, parts)


def _rows(op, a):
    return jnp.broadcast_to(op(a, axis=-1, keepdims=True), (a.shape[0], LANES))


def _rep(a, width):
    return a if width == LANES else jnp.concatenate([a] * (width // LANES), axis=1)


def _kth_largest(tile, ntiles, lo, hi, small, keep, tk, nbis):
    kf = float(keep)
    rows = lo.shape[0]
    zero = jnp.zeros((rows, LANES), F32)

    def count_ge(thr):
        thr_t = _rep(thr, tk)

        def body(t, c):
            return c + _fold(jnp.add, jnp.where(tile(t) >= thr_t, 1.0, 0.0))

        return _rows(jnp.sum, lax.fori_loop(0, ntiles, body, zero))

    def bisect(_, carry):
        lo, hi = carry
        mid = 0.5 * lo + 0.5 * hi
        ge = count_ge(mid) >= kf
        return jnp.where(ge, mid, lo), jnp.where(ge, hi, mid)

    lo, _ = lax.fori_loop(0, nbis, bisect, (lo, hi))

    def snap_body(t, c):
        st = tile(t)
        return jnp.minimum(c, _fold(jnp.minimum, jnp.where(st >= _rep(lo, tk), st, jnp.inf)))

    val = _rows(jnp.min, lax.fori_loop(0, ntiles, snap_body, jnp.full((rows, LANES), jnp.inf, F32)))

    def stats(val):
        val_t = _rep(val, tk)

        def body(t, c):
            cge, cgt, nxt = c
            st = tile(t)
            gt = st > val_t
            cge = cge + _fold(jnp.add, jnp.where(st >= val_t, 1.0, 0.0))
            cgt = cgt + _fold(jnp.add, jnp.where(gt, 1.0, 0.0))
            nxt = jnp.minimum(nxt, _fold(jnp.minimum, jnp.where(gt, st, jnp.inf)))
            return cge, cgt, nxt

        cge, cgt, nxt = lax.fori_loop(0, ntiles, body, (zero, zero, jnp.full((rows, LANES), jnp.inf, F32)))
        return _rows(jnp.sum, cge), _rows(jnp.sum, cgt), _rows(jnp.min, nxt)

    def pending(cgt):
        return jnp.sum(jnp.where((cgt >= kf) & jnp.logical_not(small), 1.0, 0.0))

    def cond(c):
        return c[4] > 0.0

    def step(c):
        val, _, cgt, nxt, _ = c
        val = jnp.where((cgt >= kf) & jnp.logical_not(small), nxt, val)
        cge, cgt, nxt = stats(val)
        return val, cge, cgt, nxt, pending(cgt)

    cge, cgt, nxt = stats(val)
    val, cge, cgt, _, _ = lax.while_loop(cond, step, (val, cge, cgt, nxt, pending(cgt)))

    tau = jnp.where(small, -F32_MAX, val)
    quota = jnp.where(small, F32_MAX, kf - cgt)
    need_fix = jnp.sum(jnp.where((cge > kf) & jnp.logical_not(small), 1.0, 0.0))
    return tau, quota, need_fix


def _drop_surplus_ties(read, write, ntiles, tau, quota, utri, tk):
    def body(t, run):
        st = read(t)
        tie = st == _rep(tau, tk)
        prefix = jnp.dot(jnp.where(tie, 1.0, 0.0).astype(BF16), utri, preferred_element_type=F32)
        over = (_rep(run, tk) + prefix) > _rep(quota, tk)
        write(t, jnp.where(tie, jnp.where(over, -jnp.inf, st), st))
        return run + _rows(jnp.sum, jnp.where(tie, 1.0, 0.0))

    lax.fori_loop(0, ntiles, body, jnp.zeros(tau.shape, F32))


def _sparse_attend(q0, idx_keys, att_keys, att_vals, qi_ref, wi_ref, qa_ref, bias_ref, utri_ref, o_ref,
                   s_ref, qim_ref, wb_ref, qam_ref, m_ref, l_ref, acc_ref,
                   *, tq, tk, keep, nbis, n_near):
    ntiles = (q0 + tq + tk - 1) // tk
    lane = lax.broadcasted_iota(jnp.int32, (tq, LANES), 1)
    lo_half = lane < (LANES // 2)

    qi = qi_ref[...].astype(F32)
    qim_ref[...] = jnp.concatenate([qi[:, h * IDX_DIM:(h + 1) * IDX_DIM] for h in range(IDX_HEADS)],
                                   axis=0).astype(BF16)
    for h in range(IDX_HEADS):
        wb_ref[h] = jnp.broadcast_to(wi_ref[:, h:h + 1], (tq, LANES))
    for j in range(ATT_HEADS // 2):
        qp = qa_ref[:, j * LANES:(j + 1) * LANES].astype(F32)
        qam_ref[j] = jnp.concatenate([jnp.where(lo_half, qp, 0.0), jnp.where(lo_half, 0.0, qp)],
                                     axis=0).astype(BF16)

    qpos = q0 + lax.broadcasted_iota(jnp.int32, (tq, tk), 0)
    kiota = lax.broadcasted_iota(jnp.int32, (tq, tk), 1)

    def score_tile(t, carry):
        mx, mn = carry
        k0 = t * tk
        x = lax.dot_general(qim_ref[...], idx_keys(t), NT_DIMS,
                            preferred_element_type=F32)
        s = jnp.zeros((tq, tk), F32)
        for h in range(IDX_HEADS):
            s = s + _rep(wb_ref[h], tk) * jnp.maximum(x[h * tq:(h + 1) * tq], 0.0)
        s_ref[t] = jnp.where(k0 + kiota <= qpos, s, -jnp.inf)
        return jnp.maximum(mx, _fold(jnp.maximum, s)), jnp.minimum(mn, _fold(jnp.minimum, s))

    mx, mn = lax.fori_loop(0, ntiles, score_tile,
                           (jnp.full((tq, LANES), -jnp.inf, F32), jnp.full((tq, LANES), jnp.inf, F32)))
    hi = _rows(jnp.max, mx)
    lo = _rows(jnp.min, mn)

    n_visible = q0 + lax.broadcasted_iota(jnp.int32, (tq, LANES), 0) + 1
    small = n_visible <= keep
    tau, quota, need_fix = _kth_largest(lambda t: s_ref[t], ntiles, lo, hi, small, keep, tk, nbis)

    @pl.when(need_fix > 0.0)
    def _():
        def write(t, val):
            s_ref[t] = val
        _drop_surplus_ties(lambda t: s_ref[t], write, ntiles, tau, quota, utri_ref[...], tk)

    m_ref[...] = jnp.full(m_ref.shape, NEG_BIG, F32)
    l_ref[...] = jnp.zeros(l_ref.shape, F32)
    acc_ref[...] = jnp.zeros(acc_ref.shape, F32)
    tau_t = _rep(tau, tk)

    def attend_tile(t, carry):
        sel = s_ref[t] >= tau_t
        variant = jnp.minimum((q0 - t * tk) // LANES, n_near)
        for j in range(ATT_HEADS // 2):
            logits = lax.dot_general(qam_ref[j], att_keys(t, j), NT_DIMS,
                                     preferred_element_type=F32) + bias_ref[j, variant]
            probs = []
            for par in range(2):
                rows = slice(par * tq, (par + 1) * tq)
                lh = logits[rows]
                m_old = m_ref[j, rows]
                m_new = jnp.maximum(m_old, _rows(jnp.max, jnp.where(sel, lh, NEG_BIG)))
                alpha = jnp.exp(m_old - m_new)
                p = jnp.where(sel, jnp.exp(lh - _rep(m_new, tk)), 0.0)
                l_ref[j, rows] = alpha * l_ref[j, rows] + _rows(jnp.sum, p)
                m_ref[j, rows] = m_new
                acc_ref[j, rows] = alpha * acc_ref[j, rows]
                probs.append(p)
            p2 = jnp.concatenate(probs, axis=0).astype(BF16)
            acc_ref[j] += jnp.dot(p2, att_vals(t, j), preferred_element_type=F32)
        return carry

    lax.fori_loop(0, ntiles, attend_tile, 0)

    for j in range(ATT_HEADS // 2):
        out = acc_ref[j] / l_ref[j]
        o_ref[:, j * LANES:(j + 1) * LANES] = jnp.where(lo_half, out[:tq], out[tq:]).astype(o_ref.dtype)


def _bias_tiles(rel_bias, tq, tk, n_near):
    table = rel_bias[_rel_bucket_np(np.arange(REL_SATURATE + 1))]
    i = np.arange(tq)[:, None]
    j = np.arange(tk)[None, :]
    idx = np.stack([np.clip(LANES * v + i - j, 0, REL_SATURATE) for v in range(n_near)]
                   + [np.full((tq, tk), REL_SATURATE)])
    b = table[idx]
    b = b.reshape(n_near + 1, tq, tk, ATT_HEADS // 2, 2)
    return jnp.transpose(b, (3, 0, 4, 1, 2)).reshape(ATT_HEADS // 2, n_near + 1, 2 * tq, tk).astype(F32)


def _n_near(tk):
    return (tk - 1 + REL_FIRST_SATURATED - 1) // LANES + 1


def _attend_scratch(tq, tk, ntiles):
    return [pltpu.VMEM((ntiles, tq, tk), F32),
            pltpu.VMEM((IDX_HEADS * tq, IDX_DIM), BF16),
            pltpu.VMEM((IDX_HEADS, tq, LANES), F32),
            pltpu.VMEM((ATT_HEADS // 2, 2 * tq, LANES), BF16),
            pltpu.VMEM((ATT_HEADS // 2, 2 * tq, LANES), F32),
            pltpu.VMEM((ATT_HEADS // 2, 2 * tq, LANES), F32),
            pltpu.VMEM((ATT_HEADS // 2, 2 * tq, LANES), F32)]


def _pattn_kernel(qi_ref, wi_ref, kk_ref, qa_ref, ka_ref, va_ref, bias_ref, utri_ref, o_ref, *scratch,
                  tq, tk, **static):
    def rows(t):
        return pl.ds(pl.multiple_of(t * tk, tk), tk)

    _sparse_attend(pl.program_id(1) * tq,
                   lambda t: kk_ref[rows(t), :IDX_DIM],
                   lambda t, j: ka_ref[rows(t), j * LANES:(j + 1) * LANES],
                   lambda t, j: va_ref[rows(t), j * LANES:(j + 1) * LANES],
                   qi_ref, wi_ref, qa_ref, bias_ref, utri_ref, o_ref, *scratch, tq=tq, tk=tk, **static)


def _prompt_attention(qi, wi, kk, qa, ka, va, rel_bias, *, tq=128, tk=256, nbis=20):
    b, t, _ = qi.shape
    tk = min(tk, t)
    keep = min(TOPK_MAX, max(t // 4, 1))
    n_near = _n_near(tk)
    bias = _bias_tiles(rel_bias, tq, tk, n_near)
    utri = jnp.asarray(np.triu(np.ones((tk, tk), np.float32)), BF16)

    def qblk(w):
        return pl.BlockSpec((None, tq, w), lambda i, j: (i, j, 0))

    def seq(w):
        return _resident((None, t, w), lambda i, j: (i, 0, 0))

    def full(a):
        return _resident(a.shape, lambda i, j: (0,) * a.ndim)

    kern = functools.partial(_pattn_kernel, tq=tq, tk=tk, keep=keep, nbis=nbis, n_near=n_near)
    return pl.pallas_call(
        kern,
        grid=(b, t // tq),
        in_specs=[qblk(512), qblk(IDX_HEADS), seq(LANES), qblk(512), seq(512), seq(512), full(bias), full(utri)],
        out_specs=qblk(512),
        out_shape=jax.ShapeDtypeStruct((b, t, ATT_WIDTH), BF16),
        scratch_shapes=_attend_scratch(tq, tk, t // tk),
        compiler_params=pltpu.CompilerParams(dimension_semantics=("parallel", "arbitrary"),
                                             vmem_limit_bytes=VMEM_LIMIT_BYTES),
        name="pattn",
    )(qi, wi, kk, qa, ka, va, bias, utri)


def _sattn_kernel(pt_ref, qi_ref, wi_ref, qa_ref, kin_ref, kan_ref, van_ref, bias_ref, utri_ref,
                  cidx_hbm, ck_hbm, cv_hbm, o_ref,
                  idx_buf, k_buf, v_buf, sem, *scratch, n_pages, tq, tk, **static):
    seq = pl.program_id(0)
    past = n_pages * PAGE_SIZE
    pages_per_tile = tk // PAGE_SIZE

    def page_copies(p):
        page = pt_ref[seq, p]
        rows = pl.ds(pl.multiple_of(p * PAGE_SIZE, PAGE_SIZE), PAGE_SIZE)
        return (pltpu.make_async_copy(cidx_hbm.at[page], idx_buf.at[p], sem.at[0]),
                pltpu.make_async_copy(ck_hbm.at[page], k_buf.at[rows], sem.at[1]),
                pltpu.make_async_copy(cv_hbm.at[page], v_buf.at[rows], sem.at[2]))

    def start(p, carry):
        for c in page_copies(p):
            c.start()
        return carry

    lax.fori_loop(0, n_pages, start, 0)

    pad_pages = pages_per_tile
    idx_buf[n_pages:n_pages + pad_pages] = jnp.zeros((pad_pages, PAGE_SIZE, IDX_DIM), F32)
    idx_buf[n_pages, 0:tq, :] = kin_ref[...]
    zeros = jnp.zeros((tk - 2 * tq, ATT_WIDTH), BF16)
    new_rows = jnp.zeros((tq, ATT_WIDTH), F32)
    k_buf[past:past + 2 * tq, :] = jnp.concatenate([kan_ref[...], new_rows], axis=0).astype(BF16)
    v_buf[past:past + 2 * tq, :] = jnp.concatenate([van_ref[...], new_rows], axis=0).astype(BF16)
    k_buf[past + 2 * tq:past + tk, :] = zeros
    v_buf[past + 2 * tq:past + tk, :] = zeros

    def wait(p, carry):
        for c in page_copies(p):
            c.wait()
        return carry

    lax.fori_loop(0, n_pages, wait, 0)

    def rows(t):
        return pl.ds(pl.multiple_of(t * tk, tk), tk)

    def idx_keys(t):
        pages = idx_buf[pl.ds(t * pages_per_tile, pages_per_tile)]
        return pages.reshape(tk, IDX_DIM).astype(BF16)

    _sparse_attend(past, idx_keys,
                   lambda t, j: k_buf[rows(t), j * LANES:(j + 1) * LANES],
                   lambda t, j: v_buf[rows(t), j * LANES:(j + 1) * LANES],
                   qi_ref, wi_ref, qa_ref, bias_ref, utri_ref, o_ref, *scratch, tq=tq, tk=tk, **static)


def _sample_attention(qi, wi, qa, ki_new, ka_new, va_new, cache_idx, cache_k, cache_v, page_table, rel_bias,
                      *, tk=1024, nbis=20):
    s, tq, _ = qi.shape
    n_pages = page_table.shape[1]
    past = n_pages * PAGE_SIZE
    ntiles = past // tk + 1
    keep = min(TOPK_MAX, (past + tq) // 4)
    n_near = _n_near(tk)
    bias = _bias_tiles(rel_bias, tq, tk, n_near)
    utri = jnp.asarray(np.triu(np.ones((tk, tk), np.float32)), BF16)

    def per_seq(w):
        return pl.BlockSpec((None, tq, w), lambda i, pt: (i, 0, 0))

    def full(a):
        return _resident(a.shape, lambda i, pt: (0,) * a.ndim)

    hbm = pl.BlockSpec(memory_space=pl.ANY)
    kern = functools.partial(_sattn_kernel, n_pages=n_pages, tq=tq, tk=tk, keep=keep, nbis=nbis, n_near=n_near)
    grid_spec = pltpu.PrefetchScalarGridSpec(
        num_scalar_prefetch=1,
        grid=(s,),
        in_specs=[per_seq(512), per_seq(IDX_HEADS), per_seq(512), per_seq(IDX_DIM), per_seq(512), per_seq(512),
                  full(bias), full(utri), hbm, hbm, hbm],
        out_specs=per_seq(512),
        scratch_shapes=[pltpu.VMEM((ntiles * (tk // PAGE_SIZE), PAGE_SIZE, IDX_DIM), F32),
                        pltpu.VMEM((ntiles * tk, ATT_WIDTH), BF16),
                        pltpu.VMEM((ntiles * tk, ATT_WIDTH), BF16),
                        pltpu.SemaphoreType.DMA((3,))] + _attend_scratch(tq, tk, ntiles))
    return pl.pallas_call(
        kern,
        grid_spec=grid_spec,
        out_shape=jax.ShapeDtypeStruct((s, tq, ATT_WIDTH), F32),
        compiler_params=pltpu.CompilerParams(dimension_semantics=("arbitrary",),
                                             vmem_limit_bytes=VMEM_LIMIT_BYTES),
        name="sattn",
    )(page_table, qi, wi, qa, ki_new, ka_new, va_new, bias, utri, cache_idx, cache_k, cache_v)


def _merge_kernel(x_ref, yr_ref, ya_ref, sgr_ref, sga_ref, wr_ref, wa_ref, wo_ref, n2_ref,
                  wfi_ref, wfo_ref, nf_ref, y_ref):
    m = (sgr_ref[...] * jnp.dot(yr_ref[...], wr_ref[...], preferred_element_type=F32)
         + sga_ref[...] * jnp.dot(ya_ref[...], wa_ref[...], preferred_element_type=F32))
    x = x_ref[...] + jnp.dot(m.astype(BF16), wo_ref[...], preferred_element_type=F32)
    h = (x * lax.rsqrt(jnp.mean(x * x, axis=-1, keepdims=True) + NORM_EPS) * n2_ref[...]).astype(BF16)
    u = jnp.dot(h, wfi_ref[:, :FFN_HIDDEN], preferred_element_type=F32)
    gt = jnp.dot(h, wfi_ref[:, FFN_HIDDEN:], preferred_element_type=F32)
    x = x + jnp.dot((jax.nn.silu(gt) * u).astype(BF16), wfo_ref[...], preferred_element_type=F32)
    y_ref[...] = x * lax.rsqrt(jnp.mean(x * x, axis=-1, keepdims=True) + NORM_EPS) * nf_ref[...]


def _merge(x, yr, ya, sgr, sga, wr, wa, wo, n2, wfi, wfo, nf, *, tm):
    rows = x.shape[0]

    def row(w):
        return pl.BlockSpec((tm, w), lambda i: (i, 0))

    def full(a):
        return _resident(a.shape, lambda i: (0,) * a.ndim)

    return pl.pallas_call(
        _merge_kernel,
        grid=(rows // tm,),
        in_specs=[row(D_MODEL), row(1024), row(ATT_WIDTH), row(D_MODEL), row(D_MODEL),
                  full(wr), full(wa), full(wo), full(n2), full(wfi), full(wfo), full(nf)],
        out_specs=row(D_MODEL),
        out_shape=jax.ShapeDtypeStruct((rows, D_MODEL), F32),
        compiler_params=pltpu.CompilerParams(dimension_semantics=("parallel",),
                                             vmem_limit_bytes=VMEM_LIMIT_BYTES),
        name="merge",
    )(x, yr, ya, sgr, sga, wr, wa, wo, n2, wfi, wfo, nf)


_W_QI_END = 5120
_W_KI_END = _W_QI_END + IDX_DIM
_W_WI_END = _W_KI_END + IDX_HEADS


def _prep_weights(w_in, idx_ln_g, idx_ln_b):
    wm = jnp.concatenate([w_in[:, :_W_QI_END], w_in[:, _W_WI_END:]], axis=1).astype(BF16)
    wk = jnp.concatenate([w_in[:, _W_QI_END:_W_KI_END]] * 2, axis=1).astype(BF16)
    ww = jnp.pad(w_in[:, _W_KI_END:_W_WI_END], ((0, 0), (0, LANES - IDX_HEADS))).astype(BF16)
    lng = jnp.concatenate([idx_ln_g] * 2)[None, :]
    lnb = jnp.concatenate([idx_ln_b] * 2)[None, :]
    return wm, wk, ww, lng, lnb


def _rot_tables(pos):
    inv_freq = ROPE_BASE ** (-jnp.arange(0, RET_QK_DIM, 2, dtype=F32) / RET_QK_DIM)
    ang = pos.astype(F32)[:, None] * inv_freq[None, :]
    cos = jnp.cos(ang)
    sin = jnp.sin(ang)
    return jnp.concatenate([cos, cos], axis=1), jnp.concatenate([-sin, sin], axis=1)


def _project_group(x, pos_tile, norm1_g, proj_w, *, tm, act_dtype):
    b, t, _ = x.shape
    cos_t, sin_t = _rot_tables(pos_tile)
    outs = _proj(x.reshape(b * t, D_MODEL), norm1_g[None, :], *proj_w[:3], cos_t, sin_t, *proj_w[3:],
                 tm=tm, act_dtype=act_dtype)
    names = ("qr", "kr", "vr", "gr", "qa", "ka", "va", "kab", "vab", "qi", "ki", "kk", "wi", "sgr", "sga")
    return {n: o.reshape(b, t, o.shape[-1]) for n, o in zip(names, outs)}


def _finish_group(x, p, y_ret, y_att, merge_w, *, tm):
    b, t, _ = x.shape
    flat = lambda a: a.reshape(b * t, a.shape[-1])
    y = _merge(flat(x), flat(y_ret), flat(y_att), flat(p["sgr"]), flat(p["sga"]), *merge_w, tm=tm)
    return y.reshape(b, t, D_MODEL)


def kernel(x_prompt, x_sample, cache_k, cache_v, cache_idx_k, state_ret, page_table, norm1_g, w_in,
           idx_ln_g, idx_ln_b, w_ret_branch, w_att_branch, w_out, norm2_g, w_ffn_in, w_ffn_out,
           rel_bias, norm_f_g):
    depth = w_in.shape[0]
    assert depth == 1
    l = 0
    b, t, _ = x_prompt.shape
    db, dt, _ = x_sample.shape
    past = page_table.shape[1] * PAGE_SIZE
    proj_w = _prep_weights(w_in[l], idx_ln_g[l], idx_ln_b[l])
    merge_w = (w_ret_branch[l].astype(BF16), w_att_branch[l].astype(BF16), w_out[l].astype(BF16),
               norm2_g[l][None, :], w_ffn_in[l].astype(BF16), w_ffn_out[l].astype(BF16), norm_f_g[None, :])

    tm = 256
    pp = _project_group(x_prompt, jnp.arange(t, dtype=jnp.int32), norm1_g[l], proj_w, tm=tm, act_dtype=BF16)
    s0 = jnp.zeros((b, RET_HEADS, RET_QK_DIM, RET_V_DIM), F32)
    yr_p, s_p = _retention(pp["qr"], pp["kr"], pp["vr"], pp["gr"], s0, RET_CHUNK)
    ya_p = _prompt_attention(pp["qi"], pp["wi"], pp["kk"], pp["qa"], pp["kab"], pp["vab"], rel_bias)
    y_prompt = _finish_group(x_prompt, pp, yr_p, ya_p, merge_w, tm=tm)

    tm_s = min(tm, db * dt)
    ps = _project_group(x_sample, past + jnp.arange(tm_s, dtype=jnp.int32) % dt, norm1_g[l], proj_w,
                        tm=tm_s, act_dtype=F32)
    yr_s, s_s = _retention(ps["qr"], ps["kr"], ps["vr"], ps["gr"], state_ret[l],
                           RET_CHUNK if dt % RET_CHUNK == 0 else dt)
    n_pool = cache_k.shape[1]
    ya_s = _sample_attention(ps["qi"].astype(F32), ps["wi"], ps["qa"].astype(F32), ps["ki"], ps["ka"], ps["va"],
                             cache_idx_k[l],
                             cache_k[l].reshape(n_pool, PAGE_SIZE, ATT_WIDTH).astype(BF16),
                             cache_v[l].reshape(n_pool, PAGE_SIZE, ATT_WIDTH).astype(BF16),
                             page_table, rel_bias)
    y_sample = _finish_group(x_sample, ps, yr_s, ya_s.astype(BF16), merge_w, tm=tm_s)

    heads = lambda a: a.reshape(a.shape[0], a.shape[1], ATT_HEADS, ATT_HEAD_DIM)[None]
    return (y_prompt, y_sample,
            heads(pp["ka"]), heads(pp["va"]), pp["ki"][None], s_p[None],
            heads(ps["ka"]), heads(ps["va"]), ps["ki"][None], s_s[None])
```

```python
import functools
import math

import numpy as np
import jax
import jax.numpy as jnp
from jax import lax
from jax.experimental import pallas as pl
from jax.experimental.pallas import tpu as pltpu

F32 = jnp.float32
BF16 = jnp.bfloat16

D_MODEL = 1024
PAGE_SIZE = 128
RET_HEADS = 4
RET_QK_DIM = 128
RET_V_DIM = 256
RET_CHUNK = 128
ROPE_BASE = 10000.0
ATT_HEADS = 8
ATT_HEAD_DIM = 64
ATT_WIDTH = ATT_HEADS * ATT_HEAD_DIM
IDX_HEADS = 8
IDX_DIM = 64
TOPK_MAX = 256
REL_BUCKETS = 32
REL_MAX_EXACT = 16
REL_MAX_DIST = 128
REL_SATURATE = 128
REL_FIRST_SATURATED = 113
FFN_HIDDEN = 2816
NORM_EPS = 1e-6
LN_EPS = 1e-5

LANES = 128
VMEM_LIMIT_BYTES = 56 * 1024 * 1024

NEG_BIG = -1e30
F32_MAX = float(np.finfo(np.float32).max)

NT_DIMS = (((1,), (1,)), ((), ()))


def _resident(shape, index_map):
    return pl.BlockSpec(shape, index_map, pipeline_mode=pl.Buffered(1))


_C_QR, _C_KR, _C_VR, _C_GR = 0, 512, 1024, 2048
_C_QA, _C_KA, _C_VA, _C_QI = 3072, 3584, 4096, 4608
_C_SGR, _C_SGA, _C_END = 5120, 6144, 7168


def _proj_kernel(x_ref, g_ref, wm_ref, wk_ref, ww_ref, cos_ref, sin_ref, lng_ref, lnb_ref,
                 qr_ref, kr_ref, vr_ref, gr_ref, qa_ref, ka_ref, va_ref, kab_ref, vab_ref,
                 qi_ref, ki_ref, kk_ref, wi_ref, sgr_ref, sga_ref):
    x = x_ref[...]
    h = (x * lax.rsqrt(jnp.mean(x * x, axis=-1, keepdims=True) + NORM_EPS) * g_ref[...]).astype(BF16)

    def mm(a, b):
        return jnp.dot(h, wm_ref[:, a:b], preferred_element_type=F32)

    cos = cos_ref[...]
    sin = sin_ref[...]
    q = mm(_C_QR, _C_KR)
    k = mm(_C_KR, _C_VR)
    for hd in range(RET_HEADS):
        sl = slice(hd * RET_QK_DIM, (hd + 1) * RET_QK_DIM)
        qh = q[:, sl]
        kh = k[:, sl]
        qr_ref[:, sl] = (qh * cos + pltpu.roll(qh, RET_QK_DIM // 2, 1) * sin).astype(qr_ref.dtype)
        kr_ref[:, sl] = ((kh * cos + pltpu.roll(kh, RET_QK_DIM // 2, 1) * sin)
                         * (RET_QK_DIM ** -0.5)).astype(kr_ref.dtype)
    vr_ref[...] = mm(_C_VR, _C_GR).astype(vr_ref.dtype)
    gr_ref[...] = mm(_C_GR, _C_QA)
    qa_ref[...] = (mm(_C_QA, _C_KA) * (ATT_HEAD_DIM ** -0.5)).astype(BF16)
    ka = mm(_C_KA, _C_VA)
    ka_ref[...] = ka
    kab_ref[...] = ka.astype(BF16)
    va = mm(_C_VA, _C_QI)
    va_ref[...] = va
    vab_ref[...] = va.astype(BF16)
    qi_ref[...] = (mm(_C_QI, _C_SGR) * (IDX_DIM ** -0.5)).astype(BF16)
    sgr_ref[...] = jax.nn.sigmoid(mm(_C_SGR, _C_SGA))
    sga_ref[...] = jax.nn.sigmoid(mm(_C_SGA, _C_END))

    kd = jnp.dot(h, wk_ref[...], preferred_element_type=F32)
    mu = jnp.mean(kd, axis=-1, keepdims=True)
    var = jnp.mean(jnp.square(kd - mu), axis=-1, keepdims=True)
    kn = (kd - mu) * lax.rsqrt(var + LN_EPS) * lng_ref[...] + lnb_ref[...]
    ki_ref[...] = kn[:, :IDX_DIM]
    kk_ref[...] = kn.astype(BF16)
    wi = jnp.dot(h, ww_ref[...], preferred_element_type=F32)
    wi_ref[...] = wi[:, :IDX_HEADS] * (IDX_HEADS ** -0.5)


def _proj(x, g, wm, wk, ww, cos_t, sin_t, lng, lnb, *, tm, act_dtype):
    rows = x.shape[0]
    nrot = cos_t.shape[0] // tm
    grid = (rows // tm,)

    def row(w):
        return pl.BlockSpec((tm, w), lambda i: (i, 0))

    def full(a):
        return _resident(a.shape, lambda i: (0,) * a.ndim)

    rot = pl.BlockSpec((tm, LANES), lambda i: (i % nrot, 0))
    widths = [(512, act_dtype), (512, act_dtype), (1024, act_dtype), (1024, F32),
              (512, BF16), (512, F32), (512, F32), (512, BF16), (512, BF16),
              (512, BF16), (IDX_DIM, F32), (LANES, BF16), (IDX_HEADS, F32),
              (1024, F32), (1024, F32)]
    return pl.pallas_call(
        _proj_kernel,
        grid=grid,
        in_specs=[row(D_MODEL), full(g), full(wm), full(wk), full(ww), rot, rot, full(lng), full(lnb)],
        out_specs=[row(w) for w, _ in widths],
        out_shape=[jax.ShapeDtypeStruct((rows, w), dt) for w, dt in widths],
        compiler_params=pltpu.CompilerParams(dimension_semantics=("parallel",),
                                             vmem_limit_bytes=VMEM_LIMIT_BYTES),
        name="proj",
    )(x, g, wm, wk, ww, cos_t, sin_t, lng, lnb)


def _ret_kernel(q_ref, k_ref, v_ref, g_ref, s0_ref, dm_ref, qd_ref, kd_ref, cd_ref,
                y_ref, s_ref, *, mxu_dtype):
    @pl.when(pl.program_id(1) == 0)
    def _():
        s_ref[...] = s0_ref[...]

    for h in range(RET_HEADS):
        qk = slice(h * RET_QK_DIM, (h + 1) * RET_QK_DIM)
        vv = slice(h * RET_V_DIM, (h + 1) * RET_V_DIM)
        q = q_ref[:, qk].astype(F32)
        k = k_ref[:, qk].astype(F32)
        v = v_ref[:, vv].astype(mxu_dtype)
        state = s_ref[h]
        sc = lax.dot_general(q.astype(mxu_dtype), k.astype(mxu_dtype), NT_DIMS,
                             preferred_element_type=F32) * dm_ref[h]
        o = (jnp.dot(sc.astype(mxu_dtype), v, preferred_element_type=F32)
             + jnp.dot((q * qd_ref[h]).astype(mxu_dtype), state.astype(mxu_dtype),
                       preferred_element_type=F32))
        kdec = (k * kd_ref[h]).T.astype(mxu_dtype)
        s_ref[h] = state * cd_ref[h] + jnp.dot(kdec, v, preferred_element_type=F32)
        mu = jnp.mean(o, axis=-1, keepdims=True)
        var = jnp.mean(jnp.square(o - mu), axis=-1, keepdims=True)
        on = (o - mu) * lax.rsqrt(var + LN_EPS)
        y_ref[:, vv] = (jax.nn.silu(g_ref[:, vv]) * on).astype(y_ref.dtype)


def _retention(q, k, v, g, s0, chunk):
    b, t, _ = q.shape
    n = t // chunk
    log_g = np.log1p(-np.exp2(-5.0 - np.arange(RET_HEADS, dtype=np.float64)))
    pos = np.arange(chunk, dtype=np.float64)
    diff = pos[:, None] - pos[None, :]
    dmask = np.where(diff >= 0, np.exp(log_g[:, None, None] * np.maximum(diff, 0.0)), 0.0)
    q_dec = np.broadcast_to(np.exp(log_g[:, None] * (pos + 1.0))[..., None], (RET_HEADS, chunk, RET_QK_DIM))
    k_dec = np.broadcast_to(np.exp(log_g[:, None] * (chunk - 1.0 - pos))[..., None], (RET_HEADS, chunk, RET_QK_DIM))
    c_dec = np.broadcast_to(np.exp(log_g * chunk)[:, None, None], (RET_HEADS, 1, RET_V_DIM))
    consts = [jnp.asarray(a, F32) for a in (dmask, q_dec, k_dec, c_dec)]
    mxu_dtype = BF16 if chunk % 16 == 0 else F32

    def tok(w):
        return pl.BlockSpec((None, chunk, w), lambda i, j: (i, j, 0))

    def const(a):
        return _resident(a.shape, lambda i, j: (0,) * a.ndim)

    state = pl.BlockSpec((None, RET_HEADS, RET_QK_DIM, RET_V_DIM), lambda i, j: (i, 0, 0, 0))
    return pl.pallas_call(
        functools.partial(_ret_kernel, mxu_dtype=mxu_dtype),
        grid=(b, n),
        in_specs=[tok(512), tok(512), tok(1024), tok(1024), state] + [const(a) for a in consts],
        out_specs=[tok(1024), state],
        out_shape=[jax.ShapeDtypeStruct((b, t, 1024), BF16),
                   jax.ShapeDtypeStruct((b, RET_HEADS, RET_QK_DIM, RET_V_DIM), F32)],
        compiler_params=pltpu.CompilerParams(dimension_semantics=("parallel", "arbitrary"),
                                             vmem_limit_bytes=VMEM_LIMIT_BYTES),
        name="retention",
    )(q, k, v, g, s0, *consts)


def _rel_bucket_np(dist):
    dist = np.asarray(dist)
    nf = np.maximum(dist, 1).astype(np.float64)
    large = REL_MAX_EXACT + (np.log(nf / REL_MAX_EXACT) / math.log(REL_MAX_DIST / REL_MAX_EXACT)
                             * (REL_BUCKETS - REL_MAX_EXACT)).astype(np.int32)
    return np.where(dist < REL_MAX_EXACT, dist, np.minimum(large, REL_BUCKETS - 1))


def _fold(op, a):
    parts = [a[:, c * LANES:(c + 1) * LANES] for c in range(a.shape[1] // LANES)]
    return functools.reduce(op, parts)


def _rows(op, a):
    return jnp.broadcast_to(op(a, axis=-1, keepdims=True), (a.shape[0], LANES))


def _rep(a, width):
    return a if width == LANES else jnp.concatenate([a] * (width // LANES), axis=1)


def _kth_largest(tile, ntiles, lo, hi, small, keep, tk, nbis):
    kf = float(keep)
    rows = lo.shape[0]
    zero = jnp.zeros((rows, LANES), F32)

    def count_ge(thr):
        thr_t = _rep(thr, tk)

        def body(t, c):
            return c + _fold(jnp.add, jnp.where(tile(t) >= thr_t, 1.0, 0.0))

        return _rows(jnp.sum, lax.fori_loop(0, ntiles, body, zero))

    def bisect(_, carry):
        lo, hi = carry
        mid = 0.5 * lo + 0.5 * hi
        ge = count_ge(mid) >= kf
        return jnp.where(ge, mid, lo), jnp.where(ge, hi, mid)

    lo, _ = lax.fori_loop(0, nbis, bisect, (lo, hi))

    def snap_body(t, c):
        st = tile(t)
        return jnp.minimum(c, _fold(jnp.minimum, jnp.where(st >= _rep(lo, tk), st, jnp.inf)))

    val = _rows(jnp.min, lax.fori_loop(0, ntiles, snap_body, jnp.full((rows, LANES), jnp.inf, F32)))

    def stats(val):
        val_t = _rep(val, tk)

        def body(t, c):
            cge, cgt, nxt = c
            st = tile(t)
            gt = st > val_t
            cge = cge + _fold(jnp.add, jnp.where(st >= val_t, 1.0, 0.0))
            cgt = cgt + _fold(jnp.add, jnp.where(gt, 1.0, 0.0))
            nxt = jnp.minimum(nxt, _fold(jnp.minimum, jnp.where(gt, st, jnp.inf)))
            return cge, cgt, nxt

        cge, cgt, nxt = lax.fori_loop(0, ntiles, body, (zero, zero, jnp.full((rows, LANES), jnp.inf, F32)))
        return _rows(jnp.sum, cge), _rows(jnp.sum, cgt), _rows(jnp.min, nxt)

    def pending(cgt):
        return jnp.sum(jnp.where((cgt >= kf) & jnp.logical_not(small), 1.0, 0.0))

    def cond(c):
        return c[4] > 0.0

    def step(c):
        val, _, cgt, nxt, _ = c
        val = jnp.where((cgt >= kf) & jnp.logical_not(small), nxt, val)
        cge, cgt, nxt = stats(val)
        return val, cge, cgt, nxt, pending(cgt)

    cge, cgt, nxt = stats(val)
    val, cge, cgt, _, _ = lax.while_loop(cond, step, (val, cge, cgt, nxt, pending(cgt)))

    tau = jnp.where(small, -F32_MAX, val)
    quota = jnp.where(small, F32_MAX, kf - cgt)
    need_fix = jnp.sum(jnp.where((cge > kf) & jnp.logical_not(small), 1.0, 0.0))
    return tau, quota, need_fix


def _drop_surplus_ties(read, write, ntiles, tau, quota, utri, tk):
    def body(t, run):
        st = read(t)
        tie = st == _rep(tau, tk)
        prefix = jnp.dot(jnp.where(tie, 1.0, 0.0).astype(BF16), utri, preferred_element_type=F32)
        over = (_rep(run, tk) + prefix) > _rep(quota, tk)
        write(t, jnp.where(tie, jnp.where(over, -jnp.inf, st), st))
        return run + _rows(jnp.sum, jnp.where(tie, 1.0, 0.0))

    lax.fori_loop(0, ntiles, body, jnp.zeros(tau.shape, F32))


def _sparse_attend(q0, idx_keys, att_keys, att_vals, qi_ref, wi_ref, qa_ref, bias_ref, utri_ref, o_ref,
                   s_ref, qim_ref, wb_ref, qam_ref, m_ref, l_ref, acc_ref,
                   *, tq, tk, keep, nbis, n_near):
    ntiles = (q0 + tq + tk - 1) // tk
    lane = lax.broadcasted_iota(jnp.int32, (tq, LANES), 1)
    lo_half = lane < (LANES // 2)

    qi = qi_ref[...].astype(F32)
    qim_ref[...] = jnp.concatenate([qi[:, h * IDX_DIM:(h + 1) * IDX_DIM] for h in range(IDX_HEADS)],
                                   axis=0).astype(BF16)
    for h in range(IDX_HEADS):
        wb_ref[h] = jnp.broadcast_to(wi_ref[:, h:h + 1], (tq, LANES))
    for j in range(ATT_HEADS // 2):
        qp = qa_ref[:, j * LANES:(j + 1) * LANES].astype(F32)
        qam_ref[j] = jnp.concatenate([jnp.where(lo_half, qp, 0.0), jnp.where(lo_half, 0.0, qp)],
                                     axis=0).astype(BF16)

    qpos = q0 + lax.broadcasted_iota(jnp.int32, (tq, tk), 0)
    kiota = lax.broadcasted_iota(jnp.int32, (tq, tk), 1)

    def score_tile(t, carry):
        mx, mn = carry
        k0 = t * tk
        x = lax.dot_general(qim_ref[...], idx_keys(t), NT_DIMS,
                            preferred_element_type=F32)
        s = jnp.zeros((tq, tk), F32)
        for h in range(IDX_HEADS):
            s = s + _rep(wb_ref[h], tk) * jnp.maximum(x[h * tq:(h + 1) * tq], 0.0)
        s_ref[t] = jnp.where(k0 + kiota <= qpos, s, -jnp.inf)
        return jnp.maximum(mx, _fold(jnp.maximum, s)), jnp.minimum(mn, _fold(jnp.minimum, s))

    mx, mn = lax.fori_loop(0, ntiles, score_tile,
                           (jnp.full((tq, LANES), -jnp.inf, F32), jnp.full((tq, LANES), jnp.inf, F32)))
    hi = _rows(jnp.max, mx)
    lo = _rows(jnp.min, mn)

    n_visible = q0 + lax.broadcasted_iota(jnp.int32, (tq, LANES), 0) + 1
    small = n_visible <= keep
    tau, quota, need_fix = _kth_largest(lambda t: s_ref[t], ntiles, lo, hi, small, keep, tk, nbis)

    @pl.when(need_fix > 0.0)
    def _():
        def write(t, val):
            s_ref[t] = val
        _drop_surplus_ties(lambda t: s_ref[t], write, ntiles, tau, quota, utri_ref[...], tk)

    m_ref[...] = jnp.full(m_ref.shape, NEG_BIG, F32)
    l_ref[...] = jnp.zeros(l_ref.shape, F32)
    acc_ref[...] = jnp.zeros(acc_ref.shape, F32)
    tau_t = _rep(tau, tk)

    def attend_tile(t, near):
        sel = s_ref[t] >= tau_t
        variant = jnp.minimum((q0 - t * tk) // LANES, n_near)
        for j in range(ATT_HEADS // 2):
            logits = lax.dot_general(qam_ref[j], att_keys(t, j), NT_DIMS, preferred_element_type=F32)
            if near:
                logits = logits + bias_ref[variant, j]
            probs = []
            for par in range(2):
                rows = slice(par * tq, (par + 1) * tq)
                lm = jnp.where(sel, logits[rows], NEG_BIG)
                m_old = m_ref[j, rows]
                row_max = _rows(jnp.max, _fold(jnp.maximum, lm))
                if near:
                    m_new = jnp.maximum(m_old, row_max)
                    shift = m_new
                else:
                    far_bias = bias_ref[n_near, j, rows, 0:LANES]
                    m_new = jnp.maximum(m_old, row_max + far_bias)
                    shift = m_new - far_bias
                alpha = jnp.exp(m_old - m_new)
                p = jnp.exp(lm - _rep(shift, tk))
                l_ref[j, rows] = alpha * l_ref[j, rows] + _fold(jnp.add, p)
                m_ref[j, rows] = m_new
                acc_ref[j, rows] = alpha * acc_ref[j, rows]
                probs.append(p)
            p2 = jnp.concatenate(probs, axis=0).astype(BF16)
            acc_ref[j] += jnp.dot(p2, att_vals(t, j), preferred_element_type=F32)

    def far_tile(t, carry):
        attend_tile(t, False)
        return carry

    def near_tile(t, carry):
        attend_tile(t, True)
        return carry

    n_far = jnp.clip((q0 - LANES * n_near + tk) // tk, 0, ntiles)
    lax.fori_loop(0, n_far, far_tile, 0)
    lax.fori_loop(n_far, ntiles, near_tile, 0)

    for j in range(ATT_HEADS // 2):
        out = acc_ref[j] / _rows(jnp.sum, l_ref[j])
        o_ref[:, j * LANES:(j + 1) * LANES] = jnp.where(lo_half, out[:tq], out[tq:]).astype(o_ref.dtype)


def _bias_tiles(rel_bias, tq, tk, n_near):
    table = rel_bias[_rel_bucket_np(np.arange(REL_SATURATE + 1))].T
    period = tq + tk
    m = np.arange(period)
    q_minus_k = np.where(m < tk, -m, period - m)
    tiles = []
    for v in range(n_near):
        strip = table[:, np.clip(LANES * v + q_minus_k, 0, REL_SATURATE)]
        skew = jnp.tile(strip, (1, tq))[:, :tq * (period - 1)].reshape(ATT_HEADS, tq, period - 1)
        tiles.append(skew[:, :, :tk])
    tiles.append(jnp.broadcast_to(table[:, REL_SATURATE][:, None, None], (ATT_HEADS, tq, tk)))
    return jnp.stack(tiles).astype(F32)


def _n_near(tk):
    return (tk - 1 + REL_FIRST_SATURATED - 1) // LANES + 1


def _attend_scratch(tq, tk, ntiles):
    return [pltpu.VMEM((ntiles, tq, tk), F32),
            pltpu.VMEM((IDX_HEADS * tq, IDX_DIM), BF16),
            pltpu.VMEM((IDX_HEADS, tq, LANES), F32),
            pltpu.VMEM((ATT_HEADS // 2, 2 * tq, LANES), BF16),
            pltpu.VMEM((ATT_HEADS // 2, 2 * tq, LANES), F32),
            pltpu.VMEM((ATT_HEADS // 2, 2 * tq, LANES), F32),
            pltpu.VMEM((ATT_HEADS // 2, 2 * tq, LANES), F32)]


def _pattn_kernel(qi_ref, wi_ref, kk_ref, qa_ref, ka_ref, va_ref, bias_ref, utri_ref, o_ref, *scratch,
                  tq, tk, **static):
    def rows(t):
        return pl.ds(pl.multiple_of(t * tk, tk), tk)

    _sparse_attend(pl.program_id(1) * tq,
                   lambda t: kk_ref[rows(t), :IDX_DIM],
                   lambda t, j: ka_ref[rows(t), j * LANES:(j + 1) * LANES],
                   lambda t, j: va_ref[rows(t), j * LANES:(j + 1) * LANES],
                   qi_ref, wi_ref, qa_ref, bias_ref, utri_ref, o_ref, *scratch, tq=tq, tk=tk, **static)


def _prompt_attention(qi, wi, kk, qa, ka, va, rel_bias, *, tq=128, tk=256, nbis=20):
    b, t, _ = qi.shape
    tk = min(tk, t)
    keep = min(TOPK_MAX, max(t // 4, 1))
    n_near = _n_near(tk)
    bias = _bias_tiles(rel_bias, tq, tk, n_near).reshape(n_near + 1, ATT_HEADS // 2, 2 * tq, tk)
    utri = jnp.asarray(np.triu(np.ones((tk, tk), np.float32)), BF16)

    def qblk(w):
        return pl.BlockSpec((None, tq, w), lambda i, j: (i, j, 0))

    def seq(w):
        return _resident((None, t, w), lambda i, j: (i, 0, 0))

    def full(a):
        return _resident(a.shape, lambda i, j: (0,) * a.ndim)

    kern = functools.partial(_pattn_kernel, tq=tq, tk=tk, keep=keep, nbis=nbis, n_near=n_near)
    return pl.pallas_call(
        kern,
        grid=(b, t // tq),
        in_specs=[qblk(512), qblk(IDX_HEADS), seq(LANES), qblk(512), seq(512), seq(512), full(bias), full(utri)],
        out_specs=qblk(512),
        out_shape=jax.ShapeDtypeStruct((b, t, ATT_WIDTH), BF16),
        scratch_shapes=_attend_scratch(tq, tk, t // tk),
        compiler_params=pltpu.CompilerParams(dimension_semantics=("parallel", "arbitrary"),
                                             vmem_limit_bytes=VMEM_LIMIT_BYTES),
        name="pattn",
    )(qi, wi, kk, qa, ka, va, bias, utri)


def _sattn_kernel(pt_ref, qi_ref, wi_ref, qa_ref, kin_ref, kan_ref, van_ref, bias_ref, utri_ref,
                  cidx_hbm, ck_hbm, cv_hbm, o_ref,
                  idx_buf, k_buf, v_buf, knew_buf, vnew_buf, sem,
                  s_ref, qim_ref, wb_ref, qbd_ref, m_ref, l_ref, acc_ref,
                  *, n_pages, tq, tk, keep, nbis, n_near):
    seq = pl.program_id(0)
    past = n_pages * PAGE_SIZE
    ppt = tk // PAGE_SIZE
    n_cached = past // tk
    ntiles = n_cached + 1

    def idx_copy(p):
        return pltpu.make_async_copy(cidx_hbm.at[pt_ref[seq, p]], idx_buf.at[p], sem.at[0])

    def tile_copies(t, slot):
        copies = []
        for i in range(ppt):
            page = pt_ref[seq, t * ppt + i]
            copies.append(pltpu.make_async_copy(ck_hbm.at[page], k_buf.at[slot, i], sem.at[1 + slot]))
            copies.append(pltpu.make_async_copy(cv_hbm.at[page], v_buf.at[slot, i], sem.at[3 + slot]))
        return copies

    def start_idx(p, carry):
        idx_copy(p).start()
        return carry

    lax.fori_loop(0, n_pages, start_idx, 0)
    for c in tile_copies(0, 0):
        c.start()

    @pl.when(seq == 0)
    def _():
        idx_buf[n_pages:n_pages + ppt] = jnp.zeros((ppt, IDX_DIM, PAGE_SIZE), F32)
        knew_buf[...] = jnp.zeros(knew_buf.shape, F32)
        vnew_buf[...] = jnp.zeros(vnew_buf.shape, F32)

    idx_buf[n_pages, :, 0:tq] = kin_ref[...]
    knew_buf[:, 0:tq] = kan_ref[...]
    vnew_buf[:, 0:tq] = van_ref[...]

    qi = qi_ref[...]
    qim_ref[...] = jnp.concatenate([qi[:, h * IDX_DIM:(h + 1) * IDX_DIM] for h in range(IDX_HEADS)],
                                   axis=0).astype(BF16)
    for h in range(IDX_HEADS):
        wb_ref[h] = jnp.broadcast_to(wi_ref[:, h:h + 1], (tq, LANES))
    head_of_lane = lax.broadcasted_iota(jnp.int32, (tq, ATT_WIDTH), 1) // ATT_HEAD_DIM
    qa = qa_ref[...]
    qbd_ref[...] = jnp.concatenate([jnp.where(head_of_lane == h, qa, 0.0) for h in range(ATT_HEADS)],
                                   axis=0).astype(BF16)

    def wait_idx(p, carry):
        idx_copy(p).wait()
        return carry

    lax.fori_loop(0, n_pages, wait_idx, 0)

    def lanes_of(pages):
        return jnp.concatenate([pages[i] for i in range(ppt)], axis=1)

    def scores(keys_t):
        x = jnp.dot(qim_ref[...], keys_t.astype(BF16), preferred_element_type=F32)
        s = jnp.zeros((tq, tk), F32)
        for h in range(IDX_HEADS):
            s = s + _rep(wb_ref[h], tk) * jnp.maximum(x[h * tq:(h + 1) * tq], 0.0)
        return s

    def score_tile(t, carry):
        mx, mn = carry
        s = scores(lanes_of(idx_buf[pl.ds(t * ppt, ppt)]))
        s_ref[t] = s
        return jnp.maximum(mx, _fold(jnp.maximum, s)), jnp.minimum(mn, _fold(jnp.minimum, s))

    mx, mn = lax.fori_loop(0, n_cached, score_tile,
                           (jnp.full((tq, LANES), -jnp.inf, F32), jnp.full((tq, LANES), jnp.inf, F32)))
    s_new = scores(lanes_of(idx_buf[n_pages:n_pages + ppt]))
    new_visible = (lax.broadcasted_iota(jnp.int32, (tq, tk), 1)
                   <= lax.broadcasted_iota(jnp.int32, (tq, tk), 0))
    s_ref[n_cached] = jnp.where(new_visible, s_new, -jnp.inf)
    hi = _rows(jnp.max, jnp.maximum(mx, _fold(jnp.maximum, s_new)))
    lo = _rows(jnp.min, jnp.minimum(mn, _fold(jnp.minimum, s_new)))

    small = jnp.zeros((tq, LANES), jnp.bool_)
    tau, quota, need_fix = _kth_largest(lambda t: s_ref[t], ntiles, lo, hi, small, keep, tk, nbis)

    @pl.when(need_fix > 0.0)
    def _():
        def write(t, val):
            s_ref[t] = val
        _drop_surplus_ties(lambda t: s_ref[t], write, ntiles, tau, quota, utri_ref[...], tk)

    m_ref[...] = jnp.full(m_ref.shape, NEG_BIG, F32)
    l_ref[...] = jnp.zeros(l_ref.shape, F32)
    acc_ref[...] = jnp.zeros(acc_ref.shape, F32)
    tau_t = _rep(tau, tk)

    def attend(t, keys_t, vals_t, variant):
        hide = jnp.where(s_ref[t] >= tau_t, 0.0, NEG_BIG)
        lm = (jnp.dot(qbd_ref[...], keys_t.astype(BF16), preferred_element_type=F32)
              + jnp.concatenate([hide] * ATT_HEADS, axis=0))
        m_old = m_ref[...]
        if variant < n_near:
            lm = lm + bias_ref[variant]
            m_new = jnp.maximum(m_old, _rows(jnp.max, _fold(jnp.maximum, lm)))
            shift = m_new
        else:
            far_bias = bias_ref[n_near, :, 0:LANES]
            m_new = jnp.maximum(m_old, _rows(jnp.max, _fold(jnp.maximum, lm)) + far_bias)
            shift = m_new - far_bias
        alpha = jnp.exp(m_old - m_new)
        p = jnp.exp(lm - _rep(shift, tk))
        l_ref[...] = alpha * l_ref[...] + _fold(jnp.add, p)
        m_ref[...] = m_new
        pv = lax.dot_general(p.astype(BF16), vals_t.astype(BF16), NT_DIMS, preferred_element_type=F32)
        acc_ref[...] = _rep(alpha, ATT_WIDTH) * acc_ref[...] + pv

    def cached_tile(t, variant):
        slot = t % 2
        for c in tile_copies(t, slot):
            c.wait()
        if isinstance(t, int):
            if t + 1 < n_cached:
                for c in tile_copies(t + 1, 1 - slot):
                    c.start()
        else:
            @pl.when(t + 1 < n_cached)
            def _():
                for c in tile_copies(t + 1, 1 - slot):
                    c.start()
        attend(t, lanes_of(k_buf[slot]), lanes_of(v_buf[slot]), variant)

    variants = [(past - t * tk) // LANES for t in range(n_cached)]
    n_far = sum(1 for v in variants if v >= n_near)

    def far_tile(t, carry):
        cached_tile(t, n_near)
        return carry

    lax.fori_loop(0, n_far, far_tile, 0)
    for t in range(n_far, n_cached):
        cached_tile(t, variants[t])
    attend(n_cached, knew_buf[...], vnew_buf[...], 0)

    outs = []
    for h in range(ATT_HEADS):
        rows = slice(h * tq, (h + 1) * tq)
        denom = _rows(jnp.sum, l_ref[rows])[:, :ATT_HEAD_DIM]
        outs.append(acc_ref[rows, h * ATT_HEAD_DIM:(h + 1) * ATT_HEAD_DIM] / denom)
    o_ref[...] = jnp.concatenate(outs, axis=1)


def _sample_attention(qi, wi, qa, ki_new, ka_new, va_new, cache_idx, cache_k, cache_v, page_table, rel_bias,
                      *, tk=1024, nbis=20):
    s, tq, _ = qi.shape
    n_pages = page_table.shape[1]
    past = n_pages * PAGE_SIZE
    ppt = tk // PAGE_SIZE
    ntiles = past // tk + 1
    keep = min(TOPK_MAX, (past + tq) // 4)
    n_near = _n_near(tk)
    bias = _bias_tiles(rel_bias, tq, tk, n_near).reshape(n_near + 1, ATT_HEADS * tq, tk)
    utri = jnp.asarray(np.triu(np.ones((tk, tk), np.float32)), BF16)

    def per_seq(r, w):
        return pl.BlockSpec((None, r, w), lambda i, pt: (i, 0, 0))

    def full(a):
        return _resident(a.shape, lambda i, pt: (0,) * a.ndim)

    hbm = pl.BlockSpec(memory_space=pl.ANY)
    kern = functools.partial(_sattn_kernel, n_pages=n_pages, tq=tq, tk=tk, keep=keep, nbis=nbis, n_near=n_near)
    grid_spec = pltpu.PrefetchScalarGridSpec(
        num_scalar_prefetch=1,
        grid=(s,),
        in_specs=[per_seq(tq, 512), per_seq(tq, IDX_HEADS), per_seq(tq, 512),
                  per_seq(IDX_DIM, tq), per_seq(ATT_WIDTH, tq), per_seq(ATT_WIDTH, tq),
                  full(bias), full(utri), hbm, hbm, hbm],
        out_specs=per_seq(tq, 512),
        scratch_shapes=[pltpu.VMEM((n_pages + ppt, IDX_DIM, PAGE_SIZE), F32),
                        pltpu.VMEM((2, ppt, ATT_WIDTH, PAGE_SIZE), F32),
                        pltpu.VMEM((2, ppt, ATT_WIDTH, PAGE_SIZE), F32),
                        pltpu.VMEM((ATT_WIDTH, tk), F32),
                        pltpu.VMEM((ATT_WIDTH, tk), F32),
                        pltpu.SemaphoreType.DMA((5,)),
                        pltpu.VMEM((ntiles, tq, tk), F32),
                        pltpu.VMEM((IDX_HEADS * tq, IDX_DIM), BF16),
                        pltpu.VMEM((IDX_HEADS, tq, LANES), F32),
                        pltpu.VMEM((ATT_HEADS * tq, ATT_WIDTH), BF16),
                        pltpu.VMEM((ATT_HEADS * tq, LANES), F32),
                        pltpu.VMEM((ATT_HEADS * tq, LANES), F32),
                        pltpu.VMEM((ATT_HEADS * tq, ATT_WIDTH), F32)])
    return pl.pallas_call(
        kern,
        grid_spec=grid_spec,
        out_shape=jax.ShapeDtypeStruct((s, tq, ATT_WIDTH), F32),
        compiler_params=pltpu.CompilerParams(dimension_semantics=("arbitrary",),
                                             vmem_limit_bytes=VMEM_LIMIT_BYTES),
        name="sattn",
    )(page_table, qi, wi, qa, ki_new, ka_new, va_new, bias, utri, cache_idx, cache_k, cache_v)


def _merge_kernel(x_ref, yr_ref, ya_ref, sgr_ref, sga_ref, wr_ref, wa_ref, wo_ref, n2_ref,
                  wfi_ref, wfo_ref, nf_ref, y_ref):
    m = (sgr_ref[...] * jnp.dot(yr_ref[...], wr_ref[...], preferred_element_type=F32)
         + sga_ref[...] * jnp.dot(ya_ref[...], wa_ref[...], preferred_element_type=F32))
    x = x_ref[...] + jnp.dot(m.astype(BF16), wo_ref[...], preferred_element_type=F32)
    h = (x * lax.rsqrt(jnp.mean(x * x, axis=-1, keepdims=True) + NORM_EPS) * n2_ref[...]).astype(BF16)
    u = jnp.dot(h, wfi_ref[:, :FFN_HIDDEN], preferred_element_type=F32)
    gt = jnp.dot(h, wfi_ref[:, FFN_HIDDEN:], preferred_element_type=F32)
    x = x + jnp.dot((jax.nn.silu(gt) * u).astype(BF16), wfo_ref[...], preferred_element_type=F32)
    y_ref[...] = x * lax.rsqrt(jnp.mean(x * x, axis=-1, keepdims=True) + NORM_EPS) * nf_ref[...]


def _merge(x, yr, ya, sgr, sga, wr, wa, wo, n2, wfi, wfo, nf, *, tm):
    rows = x.shape[0]

    def row(w):
        return pl.BlockSpec((tm, w), lambda i: (i, 0))

    def full(a):
        return _resident(a.shape, lambda i: (0,) * a.ndim)

    return pl.pallas_call(
        _merge_kernel,
        grid=(rows // tm,),
        in_specs=[row(D_MODEL), row(1024), row(ATT_WIDTH), row(D_MODEL), row(D_MODEL),
                  full(wr), full(wa), full(wo), full(n2), full(wfi), full(wfo), full(nf)],
        out_specs=row(D_MODEL),
        out_shape=jax.ShapeDtypeStruct((rows, D_MODEL), F32),
        compiler_params=pltpu.CompilerParams(dimension_semantics=("parallel",),
                                             vmem_limit_bytes=VMEM_LIMIT_BYTES),
        name="merge",
    )(x, yr, ya, sgr, sga, wr, wa, wo, n2, wfi, wfo, nf)


_W_QI_END = 5120
_W_KI_END = _W_QI_END + IDX_DIM
_W_WI_END = _W_KI_END + IDX_HEADS


def _prep_weights(w_in, idx_ln_g, idx_ln_b):
    wm = jnp.concatenate([w_in[:, :_W_QI_END], w_in[:, _W_WI_END:]], axis=1).astype(BF16)
    wk = jnp.concatenate([w_in[:, _W_QI_END:_W_KI_END]] * 2, axis=1).astype(BF16)
    ww = jnp.pad(w_in[:, _W_KI_END:_W_WI_END], ((0, 0), (0, LANES - IDX_HEADS))).astype(BF16)
    lng = jnp.concatenate([idx_ln_g] * 2)[None, :]
    lnb = jnp.concatenate([idx_ln_b] * 2)[None, :]
    return wm, wk, ww, lng, lnb


def _rot_tables(pos):
    inv_freq = ROPE_BASE ** (-jnp.arange(0, RET_QK_DIM, 2, dtype=F32) / RET_QK_DIM)
    ang = pos.astype(F32)[:, None] * inv_freq[None, :]
    cos = jnp.cos(ang)
    sin = jnp.sin(ang)
    return jnp.concatenate([cos, cos], axis=1), jnp.concatenate([-sin, sin], axis=1)


def _token_minor_cache(cache_idx, cache_k, cache_v):
    n_pool = cache_k.shape[0]
    return (jnp.transpose(cache_idx, (0, 2, 1)),
            jnp.transpose(cache_k, (0, 2, 3, 1)).reshape(n_pool, ATT_WIDTH, PAGE_SIZE),
            jnp.transpose(cache_v, (0, 2, 3, 1)).reshape(n_pool, ATT_WIDTH, PAGE_SIZE))


def _project_group(x, pos_tile, norm1_g, proj_w, *, tm, act_dtype):
    b, t, _ = x.shape
    cos_t, sin_t = _rot_tables(pos_tile)
    outs = _proj(x.reshape(b * t, D_MODEL), norm1_g[None, :], *proj_w[:3], cos_t, sin_t, *proj_w[3:],
                 tm=tm, act_dtype=act_dtype)
    names = ("qr", "kr", "vr", "gr", "qa", "ka", "va", "kab", "vab", "qi", "ki", "kk", "wi", "sgr", "sga")
    return {n: o.reshape(b, t, o.shape[-1]) for n, o in zip(names, outs)}


def _finish_group(x, p, y_ret, y_att, merge_w, *, tm):
    b, t, _ = x.shape
    flat = lambda a: a.reshape(b * t, a.shape[-1])
    y = _merge(flat(x), flat(y_ret), flat(y_att), flat(p["sgr"]), flat(p["sga"]), *merge_w, tm=tm)
    return y.reshape(b, t, D_MODEL)


def kernel(x_prompt, x_sample, cache_k, cache_v, cache_idx_k, state_ret, page_table, norm1_g, w_in,
           idx_ln_g, idx_ln_b, w_ret_branch, w_att_branch, w_out, norm2_g, w_ffn_in, w_ffn_out,
           rel_bias, norm_f_g):
    depth = w_in.shape[0]
    assert depth == 1
    l = 0
    b, t, _ = x_prompt.shape
    db, dt, _ = x_sample.shape
    past = page_table.shape[1] * PAGE_SIZE
    proj_w = _prep_weights(w_in[l], idx_ln_g[l], idx_ln_b[l])
    merge_w = (w_ret_branch[l].astype(BF16), w_att_branch[l].astype(BF16), w_out[l].astype(BF16),
               norm2_g[l][None, :], w_ffn_in[l].astype(BF16), w_ffn_out[l].astype(BF16), norm_f_g[None, :])

    tm = 256
    pp = _project_group(x_prompt, jnp.arange(t, dtype=jnp.int32), norm1_g[l], proj_w, tm=tm, act_dtype=BF16)
    s0 = jnp.zeros((b, RET_HEADS, RET_QK_DIM, RET_V_DIM), F32)
    yr_p, s_p = _retention(pp["qr"], pp["kr"], pp["vr"], pp["gr"], s0, RET_CHUNK)
    ya_p = _prompt_attention(pp["qi"], pp["wi"], pp["kk"], pp["qa"], pp["kab"], pp["vab"], rel_bias)
    y_prompt = _finish_group(x_prompt, pp, yr_p, ya_p, merge_w, tm=tm)

    tm_s = min(tm, db * dt)
    ps = _project_group(x_sample, past + jnp.arange(tm_s, dtype=jnp.int32) % dt, norm1_g[l], proj_w,
                        tm=tm_s, act_dtype=F32)
    yr_s, s_s = _retention(ps["qr"], ps["kr"], ps["vr"], ps["gr"], state_ret[l],
                           RET_CHUNK if dt % RET_CHUNK == 0 else dt)
    ya_s = _sample_attention(ps["qi"].astype(F32), ps["wi"], ps["qa"].astype(F32),
                             *(jnp.swapaxes(ps[n], 1, 2) for n in ("ki", "ka", "va")),
                             *_token_minor_cache(cache_idx_k[l], cache_k[l], cache_v[l]),
                             page_table, rel_bias)
    y_sample = _finish_group(x_sample, ps, yr_s, ya_s.astype(BF16), merge_w, tm=tm_s)

    heads = lambda a: a.reshape(a.shape[0], a.shape[1], ATT_HEADS, ATT_HEAD_DIM)[None]
    return (y_prompt, y_sample,
            heads(pp["ka"]), heads(pp["va"]), pp["ki"][None], s_p[None],
            heads(ps["ka"]), heads(ps["va"]), ps["ki"][None], s_s[None])
```

```python
import functools
import math

import numpy as np
import jax
import jax.numpy as jnp
from jax import lax
from jax.experimental import pallas as pl
from jax.experimental.pallas import tpu as pltpu

F32 = jnp.float32
BF16 = jnp.bfloat16

D_MODEL = 1024
PAGE_SIZE = 128
RET_HEADS = 4
RET_QK_DIM = 128
RET_V_DIM = 256
RET_CHUNK = 128
ROPE_BASE = 10000.0
ATT_HEADS = 8
ATT_HEAD_DIM = 64
ATT_WIDTH = ATT_HEADS * ATT_HEAD_DIM
IDX_HEADS = 8
IDX_DIM = 64
TOPK_MAX = 256
REL_BUCKETS = 32
REL_MAX_EXACT = 16
REL_MAX_DIST = 128
REL_SATURATE = 128
REL_FIRST_SATURATED = 113
FFN_HIDDEN = 2816
NORM_EPS = 1e-6
LN_EPS = 1e-5

LANES = 128
VMEM_LIMIT_BYTES = 56 * 1024 * 1024
SAMPLE_KV_SLOTS = 6
BISECTIONS = 16

NEG_BIG = -1e30
F32_MAX = float(np.finfo(np.float32).max)

NT_DIMS = (((1,), (1,)), ((), ()))


def _resident(shape, index_map):
    return pl.BlockSpec(shape, index_map, pipeline_mode=pl.Buffered(1))


_C_QR, _C_KR, _C_VR, _C_GR = 0, 512, 1024, 2048
_C_QA, _C_KA, _C_VA, _C_QI = 3072, 3584, 4096, 4608
_C_SGR, _C_SGA, _C_END = 5120, 6144, 7168


def _proj_kernel(x_ref, g_ref, wm_ref, wk_ref, ww_ref, cos_ref, sin_ref, lng_ref, lnb_ref,
                 qr_ref, kr_ref, vr_ref, gr_ref, qa_ref, ka_ref, va_ref, kab_ref, vab_ref,
                 qi_ref, ki_ref, kk_ref, wi_ref, sgr_ref, sga_ref):
    x = x_ref[...]
    h = (x * lax.rsqrt(jnp.mean(x * x, axis=-1, keepdims=True) + NORM_EPS) * g_ref[...]).astype(BF16)

    def mm(a, b):
        return jnp.dot(h, wm_ref[:, a:b], preferred_element_type=F32)

    cos = cos_ref[...]
    sin = sin_ref[...]
    q = mm(_C_QR, _C_KR)
    k = mm(_C_KR, _C_VR)
    for hd in range(RET_HEADS):
        sl = slice(hd * RET_QK_DIM, (hd + 1) * RET_QK_DIM)
        qh = q[:, sl]
        kh = k[:, sl]
        qr_ref[:, sl] = (qh * cos + pltpu.roll(qh, RET_QK_DIM // 2, 1) * sin).astype(qr_ref.dtype)
        kr_ref[:, sl] = ((kh * cos + pltpu.roll(kh, RET_QK_DIM // 2, 1) * sin)
                         * (RET_QK_DIM ** -0.5)).astype(kr_ref.dtype)
    vr_ref[...] = mm(_C_VR, _C_GR).astype(vr_ref.dtype)
    gr_ref[...] = mm(_C_GR, _C_QA)
    qa_ref[...] = (mm(_C_QA, _C_KA) * (ATT_HEAD_DIM ** -0.5)).astype(BF16)
    ka = mm(_C_KA, _C_VA)
    ka_ref[...] = ka
    kab_ref[...] = ka.astype(BF16)
    va = mm(_C_VA, _C_QI)
    va_ref[...] = va
    vab_ref[...] = va.astype(BF16)
    qi_ref[...] = (mm(_C_QI, _C_SGR) * (IDX_DIM ** -0.5)).astype(BF16)
    sgr_ref[...] = jax.nn.sigmoid(mm(_C_SGR, _C_SGA))
    sga_ref[...] = jax.nn.sigmoid(mm(_C_SGA, _C_END))

    kd = jnp.dot(h, wk_ref[...], preferred_element_type=F32)
    mu = jnp.mean(kd, axis=-1, keepdims=True)
    var = jnp.mean(jnp.square(kd - mu), axis=-1, keepdims=True)
    kn = (kd - mu) * lax.rsqrt(var + LN_EPS) * lng_ref[...] + lnb_ref[...]
    ki_ref[...] = kn[:, :IDX_DIM]
    kk_ref[...] = kn.astype(BF16)
    wi = jnp.dot(h, ww_ref[...], preferred_element_type=F32)
    wi_ref[...] = wi[:, :IDX_HEADS] * (IDX_HEADS ** -0.5)


def _proj(x, g, wm, wk, ww, cos_t, sin_t, lng, lnb, *, tm, act_dtype):
    rows = x.shape[0]
    nrot = cos_t.shape[0] // tm
    grid = (rows // tm,)

    def row(w):
        return pl.BlockSpec((tm, w), lambda i: (i, 0))

    def full(a):
        return _resident(a.shape, lambda i: (0,) * a.ndim)

    rot = pl.BlockSpec((tm, LANES), lambda i: (i % nrot, 0))
    widths = [(512, act_dtype), (512, act_dtype), (1024, act_dtype), (1024, F32),
              (512, BF16), (512, F32), (512, F32), (512, BF16), (512, BF16),
              (512, BF16), (IDX_DIM, F32), (LANES, BF16), (IDX_HEADS, F32),
              (1024, F32), (1024, F32)]
    return pl.pallas_call(
        _proj_kernel,
        grid=grid,
        in_specs=[row(D_MODEL), full(g), full(wm), full(wk), full(ww), rot, rot, full(lng), full(lnb)],
        out_specs=[row(w) for w, _ in widths],
        out_shape=[jax.ShapeDtypeStruct((rows, w), dt) for w, dt in widths],
        compiler_params=pltpu.CompilerParams(dimension_semantics=("parallel",),
                                             vmem_limit_bytes=VMEM_LIMIT_BYTES),
        name="proj",
    )(x, g, wm, wk, ww, cos_t, sin_t, lng, lnb)


def _ret_kernel(q_ref, k_ref, v_ref, g_ref, s0_ref, dm_ref, qd_ref, kd_ref, cd_ref,
                y_ref, s_ref, *, mxu_dtype):
    @pl.when(pl.program_id(1) == 0)
    def _():
        s_ref[...] = s0_ref[...]

    for h in range(RET_HEADS):
        qk = slice(h * RET_QK_DIM, (h + 1) * RET_QK_DIM)
        vv = slice(h * RET_V_DIM, (h + 1) * RET_V_DIM)
        q = q_ref[:, qk].astype(F32)
        k = k_ref[:, qk].astype(F32)
        v = v_ref[:, vv].astype(mxu_dtype)
        state = s_ref[h]
        sc = lax.dot_general(q.astype(mxu_dtype), k.astype(mxu_dtype), NT_DIMS,
                             preferred_element_type=F32) * dm_ref[h]
        o = (jnp.dot(sc.astype(mxu_dtype), v, preferred_element_type=F32)
             + jnp.dot((q * qd_ref[h]).astype(mxu_dtype), state.astype(mxu_dtype),
                       preferred_element_type=F32))
        kdec = (k * kd_ref[h]).T.astype(mxu_dtype)
        s_ref[h] = state * cd_ref[h] + jnp.dot(kdec, v, preferred_element_type=F32)
        mu = jnp.mean(o, axis=-1, keepdims=True)
        var = jnp.mean(jnp.square(o - mu), axis=-1, keepdims=True)
        on = (o - mu) * lax.rsqrt(var + LN_EPS)
        y_ref[:, vv] = (jax.nn.silu(g_ref[:, vv]) * on).astype(y_ref.dtype)


def _retention(q, k, v, g, s0, chunk):
    b, t, _ = q.shape
    n = t // chunk
    log_g = np.log1p(-np.exp2(-5.0 - np.arange(RET_HEADS, dtype=np.float64)))
    pos = np.arange(chunk, dtype=np.float64)
    diff = pos[:, None] - pos[None, :]
    dmask = np.where(diff >= 0, np.exp(log_g[:, None, None] * np.maximum(diff, 0.0)), 0.0)
    q_dec = np.broadcast_to(np.exp(log_g[:, None] * (pos + 1.0))[..., None], (RET_HEADS, chunk, RET_QK_DIM))
    k_dec = np.broadcast_to(np.exp(log_g[:, None] * (chunk - 1.0 - pos))[..., None], (RET_HEADS, chunk, RET_QK_DIM))
    c_dec = np.broadcast_to(np.exp(log_g * chunk)[:, None, None], (RET_HEADS, 1, RET_V_DIM))
    consts = [jnp.asarray(a, F32) for a in (dmask, q_dec, k_dec, c_dec)]
    mxu_dtype = BF16 if chunk % 16 == 0 else F32

    def tok(w):
        return pl.BlockSpec((None, chunk, w), lambda i, j: (i, j, 0))

    def const(a):
        return _resident(a.shape, lambda i, j: (0,) * a.ndim)

    state = pl.BlockSpec((None, RET_HEADS, RET_QK_DIM, RET_V_DIM), lambda i, j: (i, 0, 0, 0))
    return pl.pallas_call(
        functools.partial(_ret_kernel, mxu_dtype=mxu_dtype),
        grid=(b, n),
        in_specs=[tok(512), tok(512), tok(1024), tok(1024), state] + [const(a) for a in consts],
        out_specs=[tok(1024), state],
        out_shape=[jax.ShapeDtypeStruct((b, t, 1024), BF16),
                   jax.ShapeDtypeStruct((b, RET_HEADS, RET_QK_DIM, RET_V_DIM), F32)],
        compiler_params=pltpu.CompilerParams(dimension_semantics=("parallel", "arbitrary"),
                                             vmem_limit_bytes=VMEM_LIMIT_BYTES),
        name="retention",
    )(q, k, v, g, s0, *consts)


def _rel_bucket_np(dist):
    dist = np.asarray(dist)
    nf = np.maximum(dist, 1).astype(np.float64)
    large = REL_MAX_EXACT + (np.log(nf / REL_MAX_EXACT) / math.log(REL_MAX_DIST / REL_MAX_EXACT)
                             * (REL_BUCKETS - REL_MAX_EXACT)).astype(np.int32)
    return np.where(dist < REL_MAX_EXACT, dist, np.minimum(large, REL_BUCKETS - 1))


def _fold(op, a):
    parts = [a[:, c * LANES:(c + 1) * LANES] for c in range(a.shape[1] // LANES)]
    return functools.reduce(op, parts)


def _rows(op, a):
    return jnp.broadcast_to(op(a, axis=-1, keepdims=True), (a.shape[0], LANES))


def _rep(a, width):
    return a if width == LANES else jnp.concatenate([a] * (width // LANES), axis=1)


def _tile_loop(ntiles, body, init):
    if not isinstance(ntiles, int):
        return lax.fori_loop(0, ntiles, body, init)
    carry = init
    for t in range(ntiles):
        carry = body(t, carry)
    return carry


def _kth_largest(tile, ntiles, lo, hi, small, keep, tk, nbis):
    kf = float(keep)
    rows = lo.shape[0]
    zero = jnp.zeros((rows, LANES), F32)

    def count_ge(thr):
        thr_t = _rep(thr, tk)

        def body(t, c):
            return c + _fold(jnp.add, jnp.where(tile(t) >= thr_t, 1.0, 0.0))

        return _rows(jnp.sum, _tile_loop(ntiles, body, zero))

    def bisect(_, carry):
        lo, hi = carry
        mid = 0.5 * lo + 0.5 * hi
        ge = count_ge(mid) >= kf
        return jnp.where(ge, mid, lo), jnp.where(ge, hi, mid)

    lo, _ = lax.fori_loop(0, nbis, bisect, (lo, hi))

    def snap_body(t, c):
        st = tile(t)
        return jnp.minimum(c, _fold(jnp.minimum, jnp.where(st >= _rep(lo, tk), st, jnp.inf)))

    val = _rows(jnp.min, _tile_loop(ntiles, snap_body, jnp.full((rows, LANES), jnp.inf, F32)))

    def stats(val):
        val_t = _rep(val, tk)

        def body(t, c):
            cge, cgt, nxt = c
            st = tile(t)
            gt = st > val_t
            cge = cge + _fold(jnp.add, jnp.where(st >= val_t, 1.0, 0.0))
            cgt = cgt + _fold(jnp.add, jnp.where(gt, 1.0, 0.0))
            nxt = jnp.minimum(nxt, _fold(jnp.minimum, jnp.where(gt, st, jnp.inf)))
            return cge, cgt, nxt

        cge, cgt, nxt = _tile_loop(ntiles, body, (zero, zero, jnp.full((rows, LANES), jnp.inf, F32)))
        return _rows(jnp.sum, cge), _rows(jnp.sum, cgt), _rows(jnp.min, nxt)

    def pending(cgt):
        return jnp.sum(jnp.where((cgt >= kf) & jnp.logical_not(small), 1.0, 0.0))

    def cond(c):
        return c[4] > 0.0

    def step(c):
        val, _, cgt, nxt, _ = c
        val = jnp.where((cgt >= kf) & jnp.logical_not(small), nxt, val)
        cge, cgt, nxt = stats(val)
        return val, cge, cgt, nxt, pending(cgt)

    cge, cgt, nxt = stats(val)
    val, cge, cgt, _, _ = lax.while_loop(cond, step, (val, cge, cgt, nxt, pending(cgt)))

    tau = jnp.where(small, -F32_MAX, val)
    quota = jnp.where(small, F32_MAX, kf - cgt)
    need_fix = jnp.sum(jnp.where((cge > kf) & jnp.logical_not(small), 1.0, 0.0))
    return tau, quota, need_fix


def _drop_surplus_ties(read, write, ntiles, tau, quota, utri, tk):
    def body(t, run):
        st = read(t)
        tie = st == _rep(tau, tk)
        prefix = jnp.dot(jnp.where(tie, 1.0, 0.0).astype(BF16), utri, preferred_element_type=F32)
        over = (_rep(run, tk) + prefix) > _rep(quota, tk)
        write(t, jnp.where(tie, jnp.where(over, -jnp.inf, st), st))
        return run + _rows(jnp.sum, jnp.where(tie, 1.0, 0.0))

    lax.fori_loop(0, ntiles, body, jnp.zeros(tau.shape, F32))


def _sparse_attend(q0, idx_keys, att_keys, att_vals, qi_ref, wi_ref, qa_ref, bias_ref, utri_ref, o_ref,
                   s_ref, qim_ref, wb_ref, qam_ref, m_ref, mfar_ref, l_ref, acc_ref, lga_ref, lgb_ref,
                   *, tq, tk, keep, nbis, n_near):
    ntiles = (q0 + tq + tk - 1) // tk
    lane = lax.broadcasted_iota(jnp.int32, (tq, LANES), 1)
    lo_half = lane < (LANES // 2)

    qi = qi_ref[...].astype(F32)
    qim_ref[...] = jnp.concatenate([qi[:, h * IDX_DIM:(h + 1) * IDX_DIM] for h in range(IDX_HEADS)],
                                   axis=0).astype(BF16)
    for h in range(IDX_HEADS):
        wb_ref[h] = jnp.broadcast_to(wi_ref[:, h:h + 1], (tq, LANES))
    for j in range(ATT_HEADS // 2):
        qp = qa_ref[:, j * LANES:(j + 1) * LANES].astype(F32)
        qam_ref[j] = jnp.concatenate([jnp.where(lo_half, qp, 0.0), jnp.where(lo_half, 0.0, qp)],
                                     axis=0).astype(BF16)

    qpos = q0 + lax.broadcasted_iota(jnp.int32, (tq, tk), 0)
    kiota = lax.broadcasted_iota(jnp.int32, (tq, tk), 1)

    def score_tile(t, carry):
        mx, mn = carry
        k0 = t * tk
        x = lax.dot_general(qim_ref[...], idx_keys(t), NT_DIMS,
                            preferred_element_type=F32)
        s = jnp.zeros((tq, tk), F32)
        for h in range(IDX_HEADS):
            s = s + _rep(wb_ref[h], tk) * jnp.maximum(x[h * tq:(h + 1) * tq], 0.0)
        s_ref[t] = jnp.where(k0 + kiota <= qpos, s, -jnp.inf)
        return jnp.maximum(mx, _fold(jnp.maximum, s)), jnp.minimum(mn, _fold(jnp.minimum, s))

    mx, mn = lax.fori_loop(0, ntiles, score_tile,
                           (jnp.full((tq, LANES), -jnp.inf, F32), jnp.full((tq, LANES), jnp.inf, F32)))
    hi = _rows(jnp.max, mx)
    lo = _rows(jnp.min, mn)

    n_visible = q0 + lax.broadcasted_iota(jnp.int32, (tq, LANES), 0) + 1
    small = n_visible <= keep
    tau, quota, need_fix = _kth_largest(lambda t: s_ref[t], ntiles, lo, hi, small, keep, tk, nbis)

    @pl.when(need_fix > 0.0)
    def _():
        def write(t, val):
            s_ref[t] = val
        _drop_surplus_ties(lambda t: s_ref[t], write, ntiles, tau, quota, utri_ref[...], tk)

    n_far = jnp.clip((q0 - LANES * n_near + tk) // tk, 0, ntiles)

    def masked_logits(t, stop, near, dst_ref):
        tc = jnp.minimum(t, ntiles - 1)
        thr = _rep(jnp.where(t < stop, tau, jnp.inf), tk)
        sel = s_ref[tc] >= thr
        for j in range(ATT_HEADS // 2):
            logits = lax.dot_general(qam_ref[j], att_keys(tc, j), NT_DIMS, preferred_element_type=F32)
            if near:
                logits = logits + bias_ref[jnp.minimum((q0 - tc * tk) // LANES, n_near), j]
            dst_ref[j] = jnp.concatenate([jnp.where(sel, logits[:tq], NEG_BIG),
                                          jnp.where(sel, logits[tq:], NEG_BIG)], axis=0)

    def sweep(start, stop, near, consume):
        masked_logits(start, stop, near, lga_ref)

        def body(i, carry):
            t = start + 2 * i
            masked_logits(t + 1, stop, near, lgb_ref)
            for j in range(ATT_HEADS // 2):
                consume(t, j, near, lga_ref[j])
            masked_logits(t + 2, stop, near, lga_ref)
            for j in range(ATT_HEADS // 2):
                consume(t + 1, j, near, lgb_ref[j])
            return carry

        lax.fori_loop(0, (stop - start + 1) // 2, body, 0)

    def take_online(t, j, near, lm):
        m_old = m_ref[j]
        row_max = _rows(jnp.max, _fold(jnp.maximum, lm))
        if near:
            m_new = jnp.maximum(m_old, row_max)
            shift = m_new
        else:
            far_bias = bias_ref[n_near, j, :, 0:LANES]
            m_new = jnp.maximum(m_old, row_max + far_bias)
            shift = m_new - far_bias
        alpha = jnp.exp(m_old - m_new)
        p = jnp.exp(lm - _rep(shift, tk))
        l_ref[j] = alpha * l_ref[j] + _fold(jnp.add, p)
        m_ref[j] = m_new
        acc_ref[j] = alpha * acc_ref[j] + jnp.dot(p.astype(BF16), att_vals(jnp.minimum(t, ntiles - 1), j),
                                                  preferred_element_type=F32)

    m_ref[...] = jnp.full(m_ref.shape, NEG_BIG, F32)
    l_ref[...] = jnp.zeros(l_ref.shape, F32)
    acc_ref[...] = jnp.zeros(acc_ref.shape, F32)
    sweep(0, n_far, False, take_online)
    sweep(n_far, ntiles, True, take_online)

    for j in range(ATT_HEADS // 2):
        out = acc_ref[j] / _rows(jnp.sum, l_ref[j])
        o_ref[:, j * LANES:(j + 1) * LANES] = jnp.where(lo_half, out[:tq], out[tq:]).astype(o_ref.dtype)


def _bias_tiles(rel_bias, tq, tk, n_near):
    table = rel_bias[_rel_bucket_np(np.arange(REL_SATURATE + 1))].T
    period = tq + tk
    m = np.arange(period)
    q_minus_k = np.where(m < tk, -m, period - m)
    tiles = []
    for v in range(n_near):
        strip = table[:, np.clip(LANES * v + q_minus_k, 0, REL_SATURATE)]
        skew = jnp.tile(strip, (1, tq))[:, :tq * (period - 1)].reshape(ATT_HEADS, tq, period - 1)
        tiles.append(skew[:, :, :tk])
    tiles.append(jnp.broadcast_to(table[:, REL_SATURATE][:, None, None], (ATT_HEADS, tq, tk)))
    return jnp.stack(tiles).astype(F32)


def _n_near(tk):
    return (tk - 1 + REL_FIRST_SATURATED - 1) // LANES + 1


def _attend_scratch(tq, tk, ntiles):
    return [pltpu.VMEM((ntiles, tq, tk), F32),
            pltpu.VMEM((IDX_HEADS * tq, IDX_DIM), BF16),
            pltpu.VMEM((IDX_HEADS, tq, LANES), F32),
            pltpu.VMEM((ATT_HEADS // 2, 2 * tq, LANES), BF16)] + [
            pltpu.VMEM((ATT_HEADS // 2, 2 * tq, LANES), F32)] * 4 + [
            pltpu.VMEM((ATT_HEADS // 2, 2 * tq, tk), F32)] * 2


def _pattn_kernel(qi_ref, wi_ref, kk_ref, qa_ref, ka_ref, va_ref, bias_ref, utri_ref, o_ref, *scratch,
                  tq, tk, **static):
    def rows(t):
        return pl.ds(pl.multiple_of(t * tk, tk), tk)

    _sparse_attend(pl.program_id(1) * tq,
                   lambda t: kk_ref[rows(t), :IDX_DIM],
                   lambda t, j: ka_ref[rows(t), j * LANES:(j + 1) * LANES],
                   lambda t, j: va_ref[rows(t), j * LANES:(j + 1) * LANES],
                   qi_ref, wi_ref, qa_ref, bias_ref, utri_ref, o_ref, *scratch, tq=tq, tk=tk, **static)


def _prompt_attention(qi, wi, kk, qa, ka, va, rel_bias, *, tq=128, tk=512, nbis=BISECTIONS):
    b, t, _ = qi.shape
    tk = min(tk, t)
    keep = min(TOPK_MAX, max(t // 4, 1))
    n_near = _n_near(tk)
    bias = _bias_tiles(rel_bias, tq, tk, n_near).reshape(n_near + 1, ATT_HEADS // 2, 2 * tq, tk)
    utri = jnp.asarray(np.triu(np.ones((tk, tk), np.float32)), BF16)

    def qblk(w):
        return pl.BlockSpec((None, tq, w), lambda i, j: (i, j, 0))

    def seq(w):
        return _resident((None, t, w), lambda i, j: (i, 0, 0))

    def full(a):
        return _resident(a.shape, lambda i, j: (0,) * a.ndim)

    kern = functools.partial(_pattn_kernel, tq=tq, tk=tk, keep=keep, nbis=nbis, n_near=n_near)
    return pl.pallas_call(
        kern,
        grid=(b, t // tq),
        in_specs=[qblk(512), qblk(IDX_HEADS), seq(LANES), qblk(512), seq(512), seq(512), full(bias), full(utri)],
        out_specs=qblk(512),
        out_shape=jax.ShapeDtypeStruct((b, t, ATT_WIDTH), BF16),
        scratch_shapes=_attend_scratch(tq, tk, t // tk),
        compiler_params=pltpu.CompilerParams(dimension_semantics=("parallel", "arbitrary"),
                                             vmem_limit_bytes=VMEM_LIMIT_BYTES),
        name="pattn",
    )(qi, wi, kk, qa, ka, va, bias, utri)


def _sattn_kernel(pt_ref, qi_ref, wi_ref, qa_ref, kin_ref, kan_ref, van_ref, bias_ref, utri_ref,
                  cidx_hbm, ck_hbm, cv_hbm, o_ref,
                  idx_buf, k_buf, v_buf, knew_buf, vnew_buf, sem,
                  s_ref, qim_ref, wb_ref, qbd_ref, m_ref, l_ref, acc_ref, lga_ref, lgb_ref,
                  *, n_pages, n_slots, tq, tk, keep, nbis, n_near):
    seq = pl.program_id(0)
    n_seq = pl.num_programs(0)
    past = n_pages * PAGE_SIZE
    ppt = tk // PAGE_SIZE
    n_cached = past // tk
    ntiles = n_cached + 1
    ibuf = seq % 2

    def idx_copy(s, p):
        return pltpu.make_async_copy(cidx_hbm.at[pt_ref[s, p]], idx_buf.at[s % 2, p], sem.at[s % 2])

    def tile_copies(s, t):
        slot = (s * n_cached + t) % n_slots
        copies = []
        for i in range(ppt):
            page = pt_ref[s, t * ppt + i]
            copies.append(pltpu.make_async_copy(ck_hbm.at[page], k_buf.at[slot, i], sem.at[2 + slot]))
            copies.append(pltpu.make_async_copy(cv_hbm.at[page], v_buf.at[slot, i], sem.at[2 + n_slots + slot]))
        return copies

    def start_idx(s):
        def body(p, carry):
            idx_copy(s, p).start()
            return carry
        lax.fori_loop(0, n_pages, body, 0)

    def start_tile(s, t):
        for c in tile_copies(s, t):
            c.start()

    @pl.when(seq == 0)
    def _():
        start_idx(0)
        for t in range(n_slots):
            start_tile(0, t)
        idx_buf[:, n_pages:n_pages + ppt] = jnp.zeros((2, ppt, IDX_DIM, PAGE_SIZE), F32)
        knew_buf[...] = jnp.zeros(knew_buf.shape, F32)
        vnew_buf[...] = jnp.zeros(vnew_buf.shape, F32)

    idx_buf[ibuf, n_pages, :, 0:tq] = kin_ref[...]
    knew_buf[:, 0:tq] = kan_ref[...]
    vnew_buf[:, 0:tq] = van_ref[...]

    qi = qi_ref[...]
    qim_ref[...] = jnp.concatenate([qi[:, h * IDX_DIM:(h + 1) * IDX_DIM] for h in range(IDX_HEADS)],
                                   axis=0).astype(BF16)
    for h in range(IDX_HEADS):
        wb_ref[h] = jnp.broadcast_to(wi_ref[:, h:h + 1], (tq, LANES))
    head_of_lane = lax.broadcasted_iota(jnp.int32, (tq, ATT_WIDTH), 1) // ATT_HEAD_DIM
    qa = qa_ref[...]
    qbd_ref[...] = jnp.concatenate([jnp.where(head_of_lane == h, qa, 0.0) for h in range(ATT_HEADS)],
                                   axis=0).astype(BF16)

    def wait_idx(p, carry):
        idx_copy(seq, p).wait()
        return carry

    lax.fori_loop(0, n_pages, wait_idx, 0)

    @pl.when(seq + 1 < n_seq)
    def _():
        start_idx(seq + 1)

    def lanes_of(pages):
        return jnp.concatenate([pages[i] for i in range(ppt)], axis=1)

    def scores(keys_t):
        x = jnp.dot(qim_ref[...], keys_t.astype(BF16), preferred_element_type=F32)
        s = jnp.zeros((tq, tk), F32)
        for h in range(IDX_HEADS):
            s = s + _rep(wb_ref[h], tk) * jnp.maximum(x[h * tq:(h + 1) * tq], 0.0)
        return s

    def score_tile(t, carry):
        mx, mn = carry
        s = scores(lanes_of(idx_buf[ibuf, pl.ds(t * ppt, ppt)]))
        s_ref[t] = s
        return jnp.maximum(mx, _fold(jnp.maximum, s)), jnp.minimum(mn, _fold(jnp.minimum, s))

    mx, mn = lax.fori_loop(0, n_cached, score_tile,
                           (jnp.full((tq, LANES), -jnp.inf, F32), jnp.full((tq, LANES), jnp.inf, F32)))
    s_new = scores(lanes_of(idx_buf[ibuf, n_pages:n_pages + ppt]))
    new_visible = (lax.broadcasted_iota(jnp.int32, (tq, tk), 1)
                   <= lax.broadcasted_iota(jnp.int32, (tq, tk), 0))
    s_ref[n_cached] = jnp.where(new_visible, s_new, -jnp.inf)
    hi = _rows(jnp.max, jnp.maximum(mx, _fold(jnp.maximum, s_new)))
    lo = _rows(jnp.min, jnp.minimum(mn, _fold(jnp.minimum, s_new)))

    small = jnp.zeros((tq, LANES), jnp.bool_)
    tau, quota, need_fix = _kth_largest(lambda t: s_ref[t], ntiles, lo, hi, small, keep, tk, nbis)

    @pl.when(need_fix > 0.0)
    def _():
        def write(t, val):
            s_ref[t] = val
        _drop_surplus_ties(lambda t: s_ref[t], write, ntiles, tau, quota, utri_ref[...], tk)

    m_ref[...] = jnp.full(m_ref.shape, NEG_BIG, F32)
    l_ref[...] = jnp.zeros(l_ref.shape, F32)
    acc_ref[...] = jnp.zeros(acc_ref.shape, F32)
    tau_t = _rep(tau, tk)

    variants = [min((past - t * tk) // LANES, n_near) for t in range(ntiles)]
    lg_refs = (lga_ref, lgb_ref)

    def slot_of(t):
        return (seq * n_cached + t) % n_slots

    def masked_logits(t):
        if t < n_cached:
            for c in tile_copies(seq, t):
                c.wait()
            keys_t = lanes_of(k_buf[slot_of(t)])
        else:
            keys_t = knew_buf[...]
        hide = jnp.where(s_ref[t] >= tau_t, 0.0, NEG_BIG)
        lm = (jnp.dot(qbd_ref[...], keys_t.astype(BF16), preferred_element_type=F32)
              + jnp.concatenate([hide] * ATT_HEADS, axis=0))
        if variants[t] < n_near:
            lm = lm + bias_ref[variants[t]]
        lg_refs[t % 2][...] = lm

    def consume(t):
        lm = lg_refs[t % 2][...]
        m_old = m_ref[...]
        row_max = _rows(jnp.max, _fold(jnp.maximum, lm))
        if variants[t] < n_near:
            m_new = jnp.maximum(m_old, row_max)
            shift = m_new
        else:
            far_bias = bias_ref[n_near, :, 0:LANES]
            m_new = jnp.maximum(m_old, row_max + far_bias)
            shift = m_new - far_bias
        alpha = jnp.exp(m_old - m_new)
        p = jnp.exp(lm - _rep(shift, tk))
        l_ref[...] = alpha * l_ref[...] + _fold(jnp.add, p)
        m_ref[...] = m_new
        vals_t = lanes_of(v_buf[slot_of(t)]) if t < n_cached else vnew_buf[...]
        pv = lax.dot_general(p.astype(BF16), vals_t.astype(BF16), NT_DIMS, preferred_element_type=F32)
        acc_ref[...] = _rep(alpha, ATT_WIDTH) * acc_ref[...] + pv

    def refill(t):
        ahead = t + n_slots
        nxt_seq = seq + ahead // n_cached

        @pl.when(nxt_seq < n_seq)
        def _():
            start_tile(nxt_seq, ahead % n_cached)

    masked_logits(0)
    for t in range(ntiles):
        if t + 1 < ntiles:
            masked_logits(t + 1)
        consume(t)
        if t < n_cached:
            refill(t)

    outs = []
    for h in range(ATT_HEADS):
        rows = slice(h * tq, (h + 1) * tq)
        denom = _rows(jnp.sum, l_ref[rows])[:, :ATT_HEAD_DIM]
        outs.append(acc_ref[rows, h * ATT_HEAD_DIM:(h + 1) * ATT_HEAD_DIM] / denom)
    o_ref[...] = jnp.concatenate(outs, axis=1)


def _sample_attention(qi, wi, qa, ki_new, ka_new, va_new, cache_idx, cache_k, cache_v, page_table, rel_bias,
                      *, tk=1024, nbis=BISECTIONS):
    s, tq, _ = qi.shape
    n_pages = page_table.shape[1]
    past = n_pages * PAGE_SIZE
    ppt = tk // PAGE_SIZE
    ntiles = past // tk + 1
    keep = min(TOPK_MAX, (past + tq) // 4)
    n_near = _n_near(tk)
    bias = _bias_tiles(rel_bias, tq, tk, n_near).reshape(n_near + 1, ATT_HEADS * tq, tk)
    utri = jnp.asarray(np.triu(np.ones((tk, tk), np.float32)), BF16)

    def per_seq(r, w):
        return pl.BlockSpec((None, r, w), lambda i, pt: (i, 0, 0))

    def full(a):
        return _resident(a.shape, lambda i, pt: (0,) * a.ndim)

    hbm = pl.BlockSpec(memory_space=pl.ANY)
    n_slots = min(SAMPLE_KV_SLOTS, past // tk)
    kern = functools.partial(_sattn_kernel, n_pages=n_pages, n_slots=n_slots, tq=tq, tk=tk, keep=keep, nbis=nbis,
                             n_near=n_near)
    grid_spec = pltpu.PrefetchScalarGridSpec(
        num_scalar_prefetch=1,
        grid=(s,),
        in_specs=[per_seq(tq, 512), per_seq(tq, IDX_HEADS), per_seq(tq, 512),
                  per_seq(IDX_DIM, tq), per_seq(ATT_WIDTH, tq), per_seq(ATT_WIDTH, tq),
                  full(bias), full(utri), hbm, hbm, hbm],
        out_specs=per_seq(tq, 512),
        scratch_shapes=[pltpu.VMEM((2, n_pages + ppt, IDX_DIM, PAGE_SIZE), F32),
                        pltpu.VMEM((n_slots, ppt, ATT_WIDTH, PAGE_SIZE), F32),
                        pltpu.VMEM((n_slots, ppt, ATT_WIDTH, PAGE_SIZE), F32),
                        pltpu.VMEM((ATT_WIDTH, tk), F32),
                        pltpu.VMEM((ATT_WIDTH, tk), F32),
                        pltpu.SemaphoreType.DMA((2 + 2 * n_slots,)),
                        pltpu.VMEM((ntiles, tq, tk), F32),
                        pltpu.VMEM((IDX_HEADS * tq, IDX_DIM), BF16),
                        pltpu.VMEM((IDX_HEADS, tq, LANES), F32),
                        pltpu.VMEM((ATT_HEADS * tq, ATT_WIDTH), BF16),
                        pltpu.VMEM((ATT_HEADS * tq, LANES), F32),
                        pltpu.VMEM((ATT_HEADS * tq, LANES), F32),
                        pltpu.VMEM((ATT_HEADS * tq, ATT_WIDTH), F32),
                        pltpu.VMEM((ATT_HEADS * tq, tk), F32),
                        pltpu.VMEM((ATT_HEADS * tq, tk), F32)])
    return pl.pallas_call(
        kern,
        grid_spec=grid_spec,
        out_shape=jax.ShapeDtypeStruct((s, tq, ATT_WIDTH), F32),
        compiler_params=pltpu.CompilerParams(dimension_semantics=("arbitrary",),
                                             vmem_limit_bytes=VMEM_LIMIT_BYTES),
        name="sattn",
    )(page_table, qi, wi, qa, ki_new, ka_new, va_new, bias, utri, cache_idx, cache_k, cache_v)


def _merge_kernel(x_ref, yr_ref, ya_ref, sgr_ref, sga_ref, wr_ref, wa_ref, wo_ref, n2_ref,
                  wfi_ref, wfo_ref, nf_ref, y_ref):
    m = (sgr_ref[...] * jnp.dot(yr_ref[...], wr_ref[...], preferred_element_type=F32)
         + sga_ref[...] * jnp.dot(ya_ref[...], wa_ref[...], preferred_element_type=F32))
    x = x_ref[...] + jnp.dot(m.astype(BF16), wo_ref[...], preferred_element_type=F32)
    h = (x * lax.rsqrt(jnp.mean(x * x, axis=-1, keepdims=True) + NORM_EPS) * n2_ref[...]).astype(BF16)
    u = jnp.dot(h, wfi_ref[:, :FFN_HIDDEN], preferred_element_type=F32)
    gt = jnp.dot(h, wfi_ref[:, FFN_HIDDEN:], preferred_element_type=F32)
    x = x + jnp.dot((jax.nn.silu(gt) * u).astype(BF16), wfo_ref[...], preferred_element_type=F32)
    y_ref[...] = x * lax.rsqrt(jnp.mean(x * x, axis=-1, keepdims=True) + NORM_EPS) * nf_ref[...]


def _merge(x, yr, ya, sgr, sga, wr, wa, wo, n2, wfi, wfo, nf, *, tm):
    rows = x.shape[0]

    def row(w):
        return pl.BlockSpec((tm, w), lambda i: (i, 0))

    def full(a):
        return _resident(a.shape, lambda i: (0,) * a.ndim)

    return pl.pallas_call(
        _merge_kernel,
        grid=(rows // tm,),
        in_specs=[row(D_MODEL), row(1024), row(ATT_WIDTH), row(D_MODEL), row(D_MODEL),
                  full(wr), full(wa), full(wo), full(n2), full(wfi), full(wfo), full(nf)],
        out_specs=row(D_MODEL),
        out_shape=jax.ShapeDtypeStruct((rows, D_MODEL), F32),
        compiler_params=pltpu.CompilerParams(dimension_semantics=("parallel",),
                                             vmem_limit_bytes=VMEM_LIMIT_BYTES),
        name="merge",
    )(x, yr, ya, sgr, sga, wr, wa, wo, n2, wfi, wfo, nf)


_W_QI_END = 5120
_W_KI_END = _W_QI_END + IDX_DIM
_W_WI_END = _W_KI_END + IDX_HEADS


def _prep_weights(w_in, idx_ln_g, idx_ln_b):
    wm = jnp.concatenate([w_in[:, :_W_QI_END], w_in[:, _W_WI_END:]], axis=1).astype(BF16)
    wk = jnp.concatenate([w_in[:, _W_QI_END:_W_KI_END]] * 2, axis=1).astype(BF16)
    ww = jnp.pad(w_in[:, _W_KI_END:_W_WI_END], ((0, 0), (0, LANES - IDX_HEADS))).astype(BF16)
    lng = jnp.concatenate([idx_ln_g] * 2)[None, :]
    lnb = jnp.concatenate([idx_ln_b] * 2)[None, :]
    return wm, wk, ww, lng, lnb


def _rot_tables(pos):
    inv_freq = ROPE_BASE ** (-jnp.arange(0, RET_QK_DIM, 2, dtype=F32) / RET_QK_DIM)
    ang = pos.astype(F32)[:, None] * inv_freq[None, :]
    cos = jnp.cos(ang)
    sin = jnp.sin(ang)
    return jnp.concatenate([cos, cos], axis=1), jnp.concatenate([-sin, sin], axis=1)


def _token_minor_cache(cache_idx, cache_k, cache_v):
    n_pool = cache_k.shape[0]
    return (jnp.transpose(cache_idx, (0, 2, 1)),
            jnp.transpose(cache_k, (0, 2, 3, 1)).reshape(n_pool, ATT_WIDTH, PAGE_SIZE),
            jnp.transpose(cache_v, (0, 2, 3, 1)).reshape(n_pool, ATT_WIDTH, PAGE_SIZE))


def _project_group(x, pos_tile, norm1_g, proj_w, *, tm, act_dtype):
    b, t, _ = x.shape
    cos_t, sin_t = _rot_tables(pos_tile)
    outs = _proj(x.reshape(b * t, D_MODEL), norm1_g[None, :], *proj_w[:3], cos_t, sin_t, *proj_w[3:],
                 tm=tm, act_dtype=act_dtype)
    names = ("qr", "kr", "vr", "gr", "qa", "ka", "va", "kab", "vab", "qi", "ki", "kk", "wi", "sgr", "sga")
    return {n: o.reshape(b, t, o.shape[-1]) for n, o in zip(names, outs)}


def _finish_group(x, p, y_ret, y_att, merge_w, *, tm):
    b, t, _ = x.shape
    flat = lambda a: a.reshape(b * t, a.shape[-1])
    y = _merge(flat(x), flat(y_ret), flat(y_att), flat(p["sgr"]), flat(p["sga"]), *merge_w, tm=tm)
    return y.reshape(b, t, D_MODEL)


def kernel(x_prompt, x_sample, cache_k, cache_v, cache_idx_k, state_ret, page_table, norm1_g, w_in,
           idx_ln_g, idx_ln_b, w_ret_branch, w_att_branch, w_out, norm2_g, w_ffn_in, w_ffn_out,
           rel_bias, norm_f_g):
    depth = w_in.shape[0]
    assert depth == 1
    l = 0
    b, t, _ = x_prompt.shape
    db, dt, _ = x_sample.shape
    past = page_table.shape[1] * PAGE_SIZE
    proj_w = _prep_weights(w_in[l], idx_ln_g[l], idx_ln_b[l])
    merge_w = (w_ret_branch[l].astype(BF16), w_att_branch[l].astype(BF16), w_out[l].astype(BF16),
               norm2_g[l][None, :], w_ffn_in[l].astype(BF16), w_ffn_out[l].astype(BF16), norm_f_g[None, :])

    tm = 256
    pp = _project_group(x_prompt, jnp.arange(t, dtype=jnp.int32), norm1_g[l], proj_w, tm=tm, act_dtype=BF16)
    s0 = jnp.zeros((b, RET_HEADS, RET_QK_DIM, RET_V_DIM), F32)
    yr_p, s_p = _retention(pp["qr"], pp["kr"], pp["vr"], pp["gr"], s0, RET_CHUNK)
    ya_p = _prompt_attention(pp["qi"], pp["wi"], pp["kk"], pp["qa"], pp["kab"], pp["vab"], rel_bias)
    y_prompt = _finish_group(x_prompt, pp, yr_p, ya_p, merge_w, tm=tm)

    tm_s = min(tm, db * dt)
    ps = _project_group(x_sample, past + jnp.arange(tm_s, dtype=jnp.int32) % dt, norm1_g[l], proj_w,
                        tm=tm_s, act_dtype=F32)
    yr_s, s_s = _retention(ps["qr"], ps["kr"], ps["vr"], ps["gr"], state_ret[l],
                           RET_CHUNK if dt % RET_CHUNK == 0 else dt)
    ya_s = _sample_attention(ps["qi"].astype(F32), ps["wi"], ps["qa"].astype(F32),
                             *(jnp.swapaxes(ps[n], 1, 2) for n in ("ki", "ka", "va")),
                             *_token_minor_cache(cache_idx_k[l], cache_k[l], cache_v[l]),
                             page_table, rel_bias)
    y_sample = _finish_group(x_sample, ps, yr_s, ya_s.astype(BF16), merge_w, tm=tm_s)

    heads = lambda a: a.reshape(a.shape[0], a.shape[1], ATT_HEADS, ATT_HEAD_DIM)[None]
    return (y_prompt, y_sample,
            heads(pp["ka"]), heads(pp["va"]), pp["ki"][None], s_p[None],
            heads(ps["ka"]), heads(ps["va"]), ps["ki"][None], s_s[None])
```

```python
import functools
import math

import numpy as np
import jax
import jax.numpy as jnp
from jax import lax
from jax.experimental import pallas as pl
from jax.experimental.pallas import tpu as pltpu

F32 = jnp.float32
BF16 = jnp.bfloat16

D_MODEL = 1024
PAGE_SIZE = 128
RET_HEADS = 4
RET_QK_DIM = 128
RET_V_DIM = 256
RET_CHUNK = 128
ROPE_BASE = 10000.0
ATT_HEADS = 8
ATT_HEAD_DIM = 64
ATT_WIDTH = ATT_HEADS * ATT_HEAD_DIM
IDX_HEADS = 8
IDX_DIM = 64
TOPK_MAX = 256
REL_BUCKETS = 32
REL_MAX_EXACT = 16
REL_MAX_DIST = 128
REL_SATURATE = 128
REL_FIRST_SATURATED = 113
FFN_HIDDEN = 2816
NORM_EPS = 1e-6
LN_EPS = 1e-5

LANES = 128
VMEM_LIMIT_BYTES = 56 * 1024 * 1024
SAMPLE_KV_SLOTS = 6
BISECTIONS = 16

NEG_BIG = -1e30
LOG2E = math.log2(math.e)
F32_MAX = float(np.finfo(np.float32).max)

NT_DIMS = (((1,), (1,)), ((), ()))


def _resident(shape, index_map):
    return pl.BlockSpec(shape, index_map, pipeline_mode=pl.Buffered(1))


_C_QR, _C_KR, _C_VR, _C_GR = 0, 512, 1024, 2048
_C_QA, _C_KA, _C_VA, _C_QI = 3072, 3584, 4096, 4608
_C_SGR, _C_SGA, _C_END = 5120, 6144, 7168


def _proj_kernel(x_ref, g_ref, wm_ref, wk_ref, ww_ref, cos_ref, sin_ref, lng_ref, lnb_ref,
                 qr_ref, kr_ref, vr_ref, gr_ref, qa_ref, ka_ref, va_ref, kab_ref, vab_ref,
                 qi_ref, ki_ref, kk_ref, wi_ref, sgr_ref, sga_ref):
    x = x_ref[...]
    h = (x * lax.rsqrt(jnp.mean(x * x, axis=-1, keepdims=True) + NORM_EPS) * g_ref[...]).astype(BF16)

    def mm(a, b):
        return jnp.dot(h, wm_ref[:, a:b], preferred_element_type=F32)

    cos = cos_ref[...]
    sin = sin_ref[...]
    q = mm(_C_QR, _C_KR)
    k = mm(_C_KR, _C_VR)
    for hd in range(RET_HEADS):
        sl = slice(hd * RET_QK_DIM, (hd + 1) * RET_QK_DIM)
        qh = q[:, sl]
        kh = k[:, sl]
        qr_ref[:, sl] = (qh * cos + pltpu.roll(qh, RET_QK_DIM // 2, 1) * sin).astype(qr_ref.dtype)
        kr_ref[:, sl] = ((kh * cos + pltpu.roll(kh, RET_QK_DIM // 2, 1) * sin)
                         * (RET_QK_DIM ** -0.5)).astype(kr_ref.dtype)
    vr_ref[...] = mm(_C_VR, _C_GR).astype(vr_ref.dtype)
    gr_ref[...] = mm(_C_GR, _C_QA)
    qa_ref[...] = (mm(_C_QA, _C_KA) * (ATT_HEAD_DIM ** -0.5 * LOG2E)).astype(BF16)
    ka = mm(_C_KA, _C_VA)
    ka_ref[...] = ka
    kab_ref[...] = ka.astype(BF16)
    va = mm(_C_VA, _C_QI)
    va_ref[...] = va
    vab_ref[...] = va.astype(BF16)
    qi_ref[...] = (mm(_C_QI, _C_SGR) * (IDX_DIM ** -0.5)).astype(BF16)
    sgr_ref[...] = jax.nn.sigmoid(mm(_C_SGR, _C_SGA))
    sga_ref[...] = jax.nn.sigmoid(mm(_C_SGA, _C_END))

    kd = jnp.dot(h, wk_ref[...], preferred_element_type=F32)
    mu = jnp.mean(kd, axis=-1, keepdims=True)
    var = jnp.mean(jnp.square(kd - mu), axis=-1, keepdims=True)
    kn = (kd - mu) * lax.rsqrt(var + LN_EPS) * lng_ref[...] + lnb_ref[...]
    ki_ref[...] = kn[:, :IDX_DIM]
    kk_ref[...] = kn.astype(BF16)
    wi = jnp.dot(h, ww_ref[...], preferred_element_type=F32)
    wi_ref[...] = wi[:, :IDX_HEADS] * (IDX_HEADS ** -0.5)


def _proj(x, g, wm, wk, ww, cos_t, sin_t, lng, lnb, *, tm, act_dtype):
    rows = x.shape[0]
    nrot = cos_t.shape[0] // tm
    grid = (rows // tm,)

    def row(w):
        return pl.BlockSpec((tm, w), lambda i: (i, 0))

    def full(a):
        return _resident(a.shape, lambda i: (0,) * a.ndim)

    rot = pl.BlockSpec((tm, LANES), lambda i: (i % nrot, 0))
    widths = [(512, act_dtype), (512, act_dtype), (1024, act_dtype), (1024, F32),
              (512, BF16), (512, F32), (512, F32), (512, BF16), (512, BF16),
              (512, BF16), (IDX_DIM, F32), (LANES, BF16), (IDX_HEADS, F32),
              (1024, F32), (1024, F32)]
    return pl.pallas_call(
        _proj_kernel,
        grid=grid,
        in_specs=[row(D_MODEL), full(g), full(wm), full(wk), full(ww), rot, rot, full(lng), full(lnb)],
        out_specs=[row(w) for w, _ in widths],
        out_shape=[jax.ShapeDtypeStruct((rows, w), dt) for w, dt in widths],
        compiler_params=pltpu.CompilerParams(dimension_semantics=("parallel",),
                                             vmem_limit_bytes=VMEM_LIMIT_BYTES),
        name="proj",
    )(x, g, wm, wk, ww, cos_t, sin_t, lng, lnb)


def _ret_kernel(q_ref, k_ref, v_ref, g_ref, s0_ref, dm_ref, qd_ref, kd_ref, cd_ref,
                y_ref, s_ref, *, mxu_dtype):
    @pl.when(pl.program_id(1) == 0)
    def _():
        s_ref[...] = s0_ref[...]

    for h in range(RET_HEADS):
        qk = slice(h * RET_QK_DIM, (h + 1) * RET_QK_DIM)
        vv = slice(h * RET_V_DIM, (h + 1) * RET_V_DIM)
        q = q_ref[:, qk].astype(F32)
        k = k_ref[:, qk].astype(F32)
        v = v_ref[:, vv].astype(mxu_dtype)
        state = s_ref[h]
        sc = lax.dot_general(q.astype(mxu_dtype), k.astype(mxu_dtype), NT_DIMS,
                             preferred_element_type=F32) * dm_ref[h]
        o = (jnp.dot(sc.astype(mxu_dtype), v, preferred_element_type=F32)
             + jnp.dot((q * qd_ref[h]).astype(mxu_dtype), state.astype(mxu_dtype),
                       preferred_element_type=F32))
        kdec = (k * kd_ref[h]).T.astype(mxu_dtype)
        s_ref[h] = state * cd_ref[h] + jnp.dot(kdec, v, preferred_element_type=F32)
        mu = jnp.mean(o, axis=-1, keepdims=True)
        var = jnp.mean(jnp.square(o - mu), axis=-1, keepdims=True)
        on = (o - mu) * lax.rsqrt(var + LN_EPS)
        y_ref[:, vv] = (jax.nn.silu(g_ref[:, vv]) * on).astype(y_ref.dtype)


def _retention(q, k, v, g, s0, chunk):
    b, t, _ = q.shape
    n = t // chunk
    log_g = np.log1p(-np.exp2(-5.0 - np.arange(RET_HEADS, dtype=np.float64)))
    pos = np.arange(chunk, dtype=np.float64)
    diff = pos[:, None] - pos[None, :]
    dmask = np.where(diff >= 0, np.exp(log_g[:, None, None] * np.maximum(diff, 0.0)), 0.0)
    q_dec = np.broadcast_to(np.exp(log_g[:, None] * (pos + 1.0))[..., None], (RET_HEADS, chunk, RET_QK_DIM))
    k_dec = np.broadcast_to(np.exp(log_g[:, None] * (chunk - 1.0 - pos))[..., None], (RET_HEADS, chunk, RET_QK_DIM))
    c_dec = np.broadcast_to(np.exp(log_g * chunk)[:, None, None], (RET_HEADS, 1, RET_V_DIM))
    consts = [jnp.asarray(a, F32) for a in (dmask, q_dec, k_dec, c_dec)]
    mxu_dtype = BF16 if chunk % 16 == 0 else F32

    def tok(w):
        return pl.BlockSpec((None, chunk, w), lambda i, j: (i, j, 0))

    def const(a):
        return _resident(a.shape, lambda i, j: (0,) * a.ndim)

    state = pl.BlockSpec((None, RET_HEADS, RET_QK_DIM, RET_V_DIM), lambda i, j: (i, 0, 0, 0))
    return pl.pallas_call(
        functools.partial(_ret_kernel, mxu_dtype=mxu_dtype),
        grid=(b, n),
        in_specs=[tok(512), tok(512), tok(1024), tok(1024), state] + [const(a) for a in consts],
        out_specs=[tok(1024), state],
        out_shape=[jax.ShapeDtypeStruct((b, t, 1024), BF16),
                   jax.ShapeDtypeStruct((b, RET_HEADS, RET_QK_DIM, RET_V_DIM), F32)],
        compiler_params=pltpu.CompilerParams(dimension_semantics=("parallel", "arbitrary"),
                                             vmem_limit_bytes=VMEM_LIMIT_BYTES),
        name="retention",
    )(q, k, v, g, s0, *consts)


def _rel_bucket_np(dist):
    dist = np.asarray(dist)
    nf = np.maximum(dist, 1).astype(np.float64)
    large = REL_MAX_EXACT + (np.log(nf / REL_MAX_EXACT) / math.log(REL_MAX_DIST / REL_MAX_EXACT)
                             * (REL_BUCKETS - REL_MAX_EXACT)).astype(np.int32)
    return np.where(dist < REL_MAX_EXACT, dist, np.minimum(large, REL_BUCKETS - 1))


def _fold(op, a):
    parts = [a[:, c * LANES:(c + 1) * LANES] for c in range(a.shape[1] // LANES)]
    return functools.reduce(op, parts)


def _rows(op, a):
    return jnp.broadcast_to(op(a, axis=-1, keepdims=True), (a.shape[0], LANES))


def _rep(a, width):
    return a if width == LANES else jnp.concatenate([a] * (width // LANES), axis=1)


def _tile_loop(ntiles, body, init):
    if not isinstance(ntiles, int):
        return lax.fori_loop(0, ntiles, body, init)
    carry = init
    for t in range(ntiles):
        carry = body(t, carry)
    return carry


def _kth_largest(tile, ntiles, lo, hi, small, keep, tk, nbis):
    kf = float(keep)
    rows = lo.shape[0]
    zero = jnp.zeros((rows, LANES), F32)

    def count_ge(thr):
        thr_t = _rep(thr, tk)

        def body(t, c):
            return c + _fold(jnp.add, jnp.where(tile(t) >= thr_t, 1.0, 0.0))

        return _rows(jnp.sum, _tile_loop(ntiles, body, zero))

    def bisect(_, carry):
        lo, hi = carry
        mid = 0.5 * lo + 0.5 * hi
        ge = count_ge(mid) >= kf
        return jnp.where(ge, mid, lo), jnp.where(ge, hi, mid)

    lo, _ = lax.fori_loop(0, nbis, bisect, (lo, hi))

    def snap_body(t, c):
        st = tile(t)
        return jnp.minimum(c, _fold(jnp.minimum, jnp.where(st >= _rep(lo, tk), st, jnp.inf)))

    val = _rows(jnp.min, _tile_loop(ntiles, snap_body, jnp.full((rows, LANES), jnp.inf, F32)))

    def stats(val):
        val_t = _rep(val, tk)

        def body(t, c):
            cge, cgt, nxt = c
            st = tile(t)
            gt = st > val_t
            cge = cge + _fold(jnp.add, jnp.where(st >= val_t, 1.0, 0.0))
            cgt = cgt + _fold(jnp.add, jnp.where(gt, 1.0, 0.0))
            nxt = jnp.minimum(nxt, _fold(jnp.minimum, jnp.where(gt, st, jnp.inf)))
            return cge, cgt, nxt

        cge, cgt, nxt = _tile_loop(ntiles, body, (zero, zero, jnp.full((rows, LANES), jnp.inf, F32)))
        return _rows(jnp.sum, cge), _rows(jnp.sum, cgt), _rows(jnp.min, nxt)

    def pending(cgt):
        return jnp.sum(jnp.where((cgt >= kf) & jnp.logical_not(small), 1.0, 0.0))

    def cond(c):
        return c[4] > 0.0

    def step(c):
        val, _, cgt, nxt, _ = c
        val = jnp.where((cgt >= kf) & jnp.logical_not(small), nxt, val)
        cge, cgt, nxt = stats(val)
        return val, cge, cgt, nxt, pending(cgt)

    cge, cgt, nxt = stats(val)
    val, cge, cgt, _, _ = lax.while_loop(cond, step, (val, cge, cgt, nxt, pending(cgt)))

    tau = jnp.where(small, -F32_MAX, val)
    quota = jnp.where(small, F32_MAX, kf - cgt)
    need_fix = jnp.sum(jnp.where((cge > kf) & jnp.logical_not(small), 1.0, 0.0))
    return tau, quota, need_fix


def _drop_surplus_ties(read, write, ntiles, tau, quota, utri, tk):
    def body(t, run):
        st = read(t)
        tie = st == _rep(tau, tk)
        prefix = jnp.dot(jnp.where(tie, 1.0, 0.0).astype(BF16), utri, preferred_element_type=F32)
        over = (_rep(run, tk) + prefix) > _rep(quota, tk)
        write(t, jnp.where(tie, jnp.where(over, -jnp.inf, st), st))
        return run + _rows(jnp.sum, jnp.where(tie, 1.0, 0.0))

    lax.fori_loop(0, ntiles, body, jnp.zeros(tau.shape, F32))


def _sparse_attend(q0, idx_keys, att_keys, att_vals, qi_ref, wi_ref, qa_ref, bias_ref, utri_ref, o_ref,
                   s_ref, qim_ref, wb_ref, qam_ref, m_ref, l_ref, acc_ref, lga_ref, lgb_ref, xa_ref, xb_ref,
                   *, tq, tk, keep, nbis, n_near):
    ntiles = (q0 + tq + tk - 1) // tk
    lane = lax.broadcasted_iota(jnp.int32, (tq, LANES), 1)
    lo_half = lane < (LANES // 2)

    qi = qi_ref[...].astype(F32)
    qim_ref[...] = jnp.concatenate([qi[:, h * IDX_DIM:(h + 1) * IDX_DIM] for h in range(IDX_HEADS)],
                                   axis=0).astype(BF16)
    for h in range(IDX_HEADS):
        wb_ref[h] = jnp.broadcast_to(wi_ref[:, h:h + 1], (tq, LANES))
    for j in range(ATT_HEADS // 2):
        qp = qa_ref[:, j * LANES:(j + 1) * LANES].astype(F32)
        qam_ref[j] = jnp.concatenate([jnp.where(lo_half, qp, 0.0), jnp.where(lo_half, 0.0, qp)],
                                     axis=0).astype(BF16)

    qpos = q0 + lax.broadcasted_iota(jnp.int32, (tq, tk), 0)
    kiota = lax.broadcasted_iota(jnp.int32, (tq, tk), 1)

    def head_products(t, dst_ref):
        dst_ref[...] = lax.dot_general(qim_ref[...], idx_keys(jnp.minimum(t, ntiles - 1)), NT_DIMS,
                                       preferred_element_type=F32)

    def reduce_heads(t, x_ref, carry):
        mx, mn = carry
        tc = jnp.minimum(t, ntiles - 1)
        s = jnp.zeros((tq, tk), F32)
        for h in range(IDX_HEADS):
            s = s + _rep(wb_ref[h], tk) * jnp.maximum(x_ref[h * tq:(h + 1) * tq, :], 0.0)
        s_ref[tc] = jnp.where(tc * tk + kiota <= qpos, s, -jnp.inf)
        return jnp.maximum(mx, _fold(jnp.maximum, s)), jnp.minimum(mn, _fold(jnp.minimum, s))

    def score_two_tiles(i, carry):
        t = 2 * i
        head_products(t + 1, xb_ref)
        carry = reduce_heads(t, xa_ref, carry)
        head_products(t + 2, xa_ref)
        return reduce_heads(t + 1, xb_ref, carry)

    head_products(0, xa_ref)
    mx, mn = lax.fori_loop(0, (ntiles + 1) // 2, score_two_tiles,
                           (jnp.full((tq, LANES), -jnp.inf, F32), jnp.full((tq, LANES), jnp.inf, F32)))
    hi = _rows(jnp.max, mx)
    lo = _rows(jnp.min, mn)

    n_visible = q0 + lax.broadcasted_iota(jnp.int32, (tq, LANES), 0) + 1
    small = n_visible <= keep
    tau, quota, need_fix = _kth_largest(lambda t: s_ref[t], ntiles, lo, hi, small, keep, tk, nbis)

    @pl.when(need_fix > 0.0)
    def _():
        def write(t, val):
            s_ref[t] = val
        _drop_surplus_ties(lambda t: s_ref[t], write, ntiles, tau, quota, utri_ref[...], tk)

    n_far = jnp.clip((q0 - LANES * n_near + tk) // tk, 0, ntiles)

    def masked_logits(t, stop, near, dst_ref):
        tc = jnp.minimum(t, ntiles - 1)
        thr = _rep(jnp.where(t < stop, tau, jnp.inf), tk)
        sel = s_ref[tc] >= thr
        for j in range(ATT_HEADS // 2):
            logits = lax.dot_general(qam_ref[j], att_keys(tc, j), NT_DIMS, preferred_element_type=F32)
            if near:
                logits = logits + bias_ref[jnp.minimum((q0 - tc * tk) // LANES, n_near), j]
            dst_ref[j] = jnp.concatenate([jnp.where(sel, logits[:tq], NEG_BIG),
                                          jnp.where(sel, logits[tq:], NEG_BIG)], axis=0)

    def sweep(start, stop, near, consume):
        masked_logits(start, stop, near, lga_ref)

        def body(i, carry):
            t = start + 2 * i
            masked_logits(t + 1, stop, near, lgb_ref)
            for j in range(ATT_HEADS // 2):
                consume(t, j, near, lga_ref[j])
            masked_logits(t + 2, stop, near, lga_ref)
            for j in range(ATT_HEADS // 2):
                consume(t + 1, j, near, lgb_ref[j])
            return carry

        lax.fori_loop(0, (stop - start + 1) // 2, body, 0)

    def take_online(t, j, near, lm):
        m_old = m_ref[j]
        row_max = _rows(jnp.max, _fold(jnp.maximum, lm))
        if near:
            m_new = jnp.maximum(m_old, row_max)
            shift = m_new
        else:
            far_bias = bias_ref[n_near, j, :, 0:LANES]
            m_new = jnp.maximum(m_old, row_max + far_bias)
            shift = m_new - far_bias
        alpha = jnp.exp2(m_old - m_new)
        p = jnp.exp2(lm - _rep(shift, tk))
        l_ref[j] = alpha * l_ref[j] + _fold(jnp.add, p)
        m_ref[j] = m_new
        acc_ref[j] = alpha * acc_ref[j] + jnp.dot(p.astype(BF16), att_vals(jnp.minimum(t, ntiles - 1), j),
                                                  preferred_element_type=F32)

    m_ref[...] = jnp.full(m_ref.shape, NEG_BIG, F32)
    l_ref[...] = jnp.zeros(l_ref.shape, F32)
    acc_ref[...] = jnp.zeros(acc_ref.shape, F32)
    sweep(0, n_far, False, take_online)
    sweep(n_far, ntiles, True, take_online)

    for j in range(ATT_HEADS // 2):
        out = acc_ref[j] / _rows(jnp.sum, l_ref[j])
        o_ref[:, j * LANES:(j + 1) * LANES] = jnp.where(lo_half, out[:tq], out[tq:]).astype(o_ref.dtype)


def _bias_tiles(rel_bias, tq, tk, n_near):
    table = rel_bias[_rel_bucket_np(np.arange(REL_SATURATE + 1))].T * LOG2E
    period = tq + tk
    m = np.arange(period)
    q_minus_k = np.where(m < tk, -m, period - m)
    tiles = []
    for v in range(n_near):
        strip = table[:, np.clip(LANES * v + q_minus_k, 0, REL_SATURATE)]
        skew = jnp.tile(strip, (1, tq))[:, :tq * (period - 1)].reshape(ATT_HEADS, tq, period - 1)
        tiles.append(skew[:, :, :tk])
    tiles.append(jnp.broadcast_to(table[:, REL_SATURATE][:, None, None], (ATT_HEADS, tq, tk)))
    return jnp.stack(tiles).astype(F32)


def _n_near(tk):
    return (tk - 1 + REL_FIRST_SATURATED - 1) // LANES + 1


def _attend_scratch(tq, tk, ntiles):
    return [pltpu.VMEM((ntiles, tq, tk), F32),
            pltpu.VMEM((IDX_HEADS * tq, IDX_DIM), BF16),
            pltpu.VMEM((IDX_HEADS, tq, LANES), F32),
            pltpu.VMEM((ATT_HEADS // 2, 2 * tq, LANES), BF16)] + [
            pltpu.VMEM((ATT_HEADS // 2, 2 * tq, LANES), F32)] * 3 + [
            pltpu.VMEM((ATT_HEADS // 2, 2 * tq, tk), F32)] * 2 + [
            pltpu.VMEM((IDX_HEADS * tq, tk), F32)] * 2


def _pattn_kernel(qi_ref, wi_ref, kk_ref, qa_ref, ka_ref, va_ref, bias_ref, utri_ref, o_ref, *scratch,
                  tq, tk, **static):
    def rows(t):
        return pl.ds(pl.multiple_of(t * tk, tk), tk)

    _sparse_attend(pl.program_id(1) * tq,
                   lambda t: kk_ref[rows(t), :IDX_DIM],
                   lambda t, j: ka_ref[rows(t), j * LANES:(j + 1) * LANES],
                   lambda t, j: va_ref[rows(t), j * LANES:(j + 1) * LANES],
                   qi_ref, wi_ref, qa_ref, bias_ref, utri_ref, o_ref, *scratch, tq=tq, tk=tk, **static)


def _prompt_attention(qi, wi, kk, qa, ka, va, rel_bias, *, tq=128, tk=512, nbis=BISECTIONS):
    b, t, _ = qi.shape
    tk = min(tk, t)
    keep = min(TOPK_MAX, max(t // 4, 1))
    n_near = _n_near(tk)
    bias = _bias_tiles(rel_bias, tq, tk, n_near).reshape(n_near + 1, ATT_HEADS // 2, 2 * tq, tk)
    utri = jnp.asarray(np.triu(np.ones((tk, tk), np.float32)), BF16)

    def qblk(w):
        return pl.BlockSpec((None, tq, w), lambda i, j: (i, j, 0))

    def seq(w):
        return _resident((None, t, w), lambda i, j: (i, 0, 0))

    def full(a):
        return _resident(a.shape, lambda i, j: (0,) * a.ndim)

    kern = functools.partial(_pattn_kernel, tq=tq, tk=tk, keep=keep, nbis=nbis, n_near=n_near)
    return pl.pallas_call(
        kern,
        grid=(b, t // tq),
        in_specs=[qblk(512), qblk(IDX_HEADS), seq(LANES), qblk(512), seq(512), seq(512), full(bias), full(utri)],
        out_specs=qblk(512),
        out_shape=jax.ShapeDtypeStruct((b, t, ATT_WIDTH), BF16),
        scratch_shapes=_attend_scratch(tq, tk, t // tk),
        compiler_params=pltpu.CompilerParams(dimension_semantics=("parallel", "arbitrary"),
                                             vmem_limit_bytes=VMEM_LIMIT_BYTES),
        name="pattn",
    )(qi, wi, kk, qa, ka, va, bias, utri)


def _sattn_kernel(pt_ref, qi_ref, wi_ref, qa_ref, kin_ref, kan_ref, van_ref, bias_ref, utri_ref,
                  cidx_hbm, ck_hbm, cv_hbm, o_ref,
                  idx_buf, k_buf, v_buf, knew_buf, vnew_buf, sem,
                  s_ref, qim_ref, wb_ref, qbd_ref, m_ref, l_ref, acc_ref, lga_ref, lgb_ref,
                  *, n_pages, n_slots, tq, tk, keep, nbis, n_near):
    seq = pl.program_id(0)
    n_seq = pl.num_programs(0)
    past = n_pages * PAGE_SIZE
    ppt = tk // PAGE_SIZE
    n_cached = past // tk
    ntiles = n_cached + 1
    ibuf = seq % 2

    def idx_copy(s, p):
        return pltpu.make_async_copy(cidx_hbm.at[pt_ref[s, p]], idx_buf.at[s % 2, p], sem.at[s % 2])

    def tile_copies(s, t):
        slot = (s * n_cached + t) % n_slots
        copies = []
        for i in range(ppt):
            page = pt_ref[s, t * ppt + i]
            copies.append(pltpu.make_async_copy(ck_hbm.at[page], k_buf.at[slot, i], sem.at[2 + slot]))
            copies.append(pltpu.make_async_copy(cv_hbm.at[page], v_buf.at[slot, i], sem.at[2 + n_slots + slot]))
        return copies

    def start_idx(s):
        def body(p, carry):
            idx_copy(s, p).start()
            return carry
        lax.fori_loop(0, n_pages, body, 0)

    def start_tile(s, t):
        for c in tile_copies(s, t):
            c.start()

    @pl.when(seq == 0)
    def _():
        start_idx(0)
        for t in range(n_slots):
            start_tile(0, t)
        idx_buf[:, n_pages:n_pages + ppt] = jnp.zeros((2, ppt, IDX_DIM, PAGE_SIZE), F32)
        knew_buf[...] = jnp.zeros(knew_buf.shape, F32)
        vnew_buf[...] = jnp.zeros(vnew_buf.shape, F32)

    idx_buf[ibuf, n_pages, :, 0:tq] = kin_ref[...]
    knew_buf[:, 0:tq] = kan_ref[...]
    vnew_buf[:, 0:tq] = van_ref[...]

    qi = qi_ref[...]
    qim_ref[...] = jnp.concatenate([qi[:, h * IDX_DIM:(h + 1) * IDX_DIM] for h in range(IDX_HEADS)],
                                   axis=0).astype(BF16)
    for h in range(IDX_HEADS):
        wb_ref[h] = jnp.broadcast_to(wi_ref[:, h:h + 1], (tq, LANES))
    head_of_lane = lax.broadcasted_iota(jnp.int32, (tq, ATT_WIDTH), 1) // ATT_HEAD_DIM
    qa = qa_ref[...]
    qbd_ref[...] = jnp.concatenate([jnp.where(head_of_lane == h, qa, 0.0) for h in range(ATT_HEADS)],
                                   axis=0).astype(BF16)

    def wait_idx(p, carry):
        idx_copy(seq, p).wait()
        return carry

    lax.fori_loop(0, n_pages, wait_idx, 0)

    @pl.when(seq + 1 < n_seq)
    def _():
        start_idx(seq + 1)

    def lanes_of(pages):
        return jnp.concatenate([pages[i] for i in range(ppt)], axis=1)

    def scores(keys_t):
        x = jnp.dot(qim_ref[...], keys_t.astype(BF16), preferred_element_type=F32)
        s = jnp.zeros((tq, tk), F32)
        for h in range(IDX_HEADS):
            s = s + _rep(wb_ref[h], tk) * jnp.maximum(x[h * tq:(h + 1) * tq], 0.0)
        return s

    def score_tile(t, carry):
        mx, mn = carry
        s = scores(lanes_of(idx_buf[ibuf, pl.ds(t * ppt, ppt)]))
        s_ref[t] = s
        return jnp.maximum(mx, _fold(jnp.maximum, s)), jnp.minimum(mn, _fold(jnp.minimum, s))

    mx, mn = lax.fori_loop(0, n_cached, score_tile,
                           (jnp.full((tq, LANES), -jnp.inf, F32), jnp.full((tq, LANES), jnp.inf, F32)))
    s_new = scores(lanes_of(idx_buf[ibuf, n_pages:n_pages + ppt]))
    new_visible = (lax.broadcasted_iota(jnp.int32, (tq, tk), 1)
                   <= lax.broadcasted_iota(jnp.int32, (tq, tk), 0))
    s_ref[n_cached] = jnp.where(new_visible, s_new, -jnp.inf)
    hi = _rows(jnp.max, jnp.maximum(mx, _fold(jnp.maximum, s_new)))
    lo = _rows(jnp.min, jnp.minimum(mn, _fold(jnp.minimum, s_new)))

    small = jnp.zeros((tq, LANES), jnp.bool_)
    tau, quota, need_fix = _kth_largest(lambda t: s_ref[t], ntiles, lo, hi, small, keep, tk, nbis)

    @pl.when(need_fix > 0.0)
    def _():
        def write(t, val):
            s_ref[t] = val
        _drop_surplus_ties(lambda t: s_ref[t], write, ntiles, tau, quota, utri_ref[...], tk)

    m_ref[...] = jnp.full(m_ref.shape, NEG_BIG, F32)
    l_ref[...] = jnp.zeros(l_ref.shape, F32)
    acc_ref[...] = jnp.zeros(acc_ref.shape, F32)
    tau_t = _rep(tau, tk)

    variants = [min((past - t * tk) // LANES, n_near) for t in range(ntiles)]
    lg_refs = (lga_ref, lgb_ref)

    def slot_of(t):
        return (seq * n_cached + t) % n_slots

    def masked_logits(t):
        if t < n_cached:
            for c in tile_copies(seq, t):
                c.wait()
            keys_t = lanes_of(k_buf[slot_of(t)])
        else:
            keys_t = knew_buf[...]
        hide = jnp.where(s_ref[t] >= tau_t, 0.0, NEG_BIG)
        lm = (jnp.dot(qbd_ref[...], keys_t.astype(BF16), preferred_element_type=F32)
              + jnp.concatenate([hide] * ATT_HEADS, axis=0))
        if variants[t] < n_near:
            lm = lm + bias_ref[variants[t]]
        lg_refs[t % 2][...] = lm

    def consume(t):
        lm = lg_refs[t % 2][...]
        m_old = m_ref[...]
        row_max = _rows(jnp.max, _fold(jnp.maximum, lm))
        if variants[t] < n_near:
            m_new = jnp.maximum(m_old, row_max)
            shift = m_new
        else:
            far_bias = bias_ref[n_near, :, 0:LANES]
            m_new = jnp.maximum(m_old, row_max + far_bias)
            shift = m_new - far_bias
        alpha = jnp.exp2(m_old - m_new)
        p = jnp.exp2(lm - _rep(shift, tk))
        l_ref[...] = alpha * l_ref[...] + _fold(jnp.add, p)
        m_ref[...] = m_new
        vals_t = lanes_of(v_buf[slot_of(t)]) if t < n_cached else vnew_buf[...]
        pv = lax.dot_general(p.astype(BF16), vals_t.astype(BF16), NT_DIMS, preferred_element_type=F32)
        acc_ref[...] = _rep(alpha, ATT_WIDTH) * acc_ref[...] + pv

    def refill(t):
        ahead = t + n_slots
        nxt_seq = seq + ahead // n_cached

        @pl.when(nxt_seq < n_seq)
        def _():
            start_tile(nxt_seq, ahead % n_cached)

    masked_logits(0)
    for t in range(ntiles):
        if t + 1 < ntiles:
            masked_logits(t + 1)
        consume(t)
        if t < n_cached:
            refill(t)

    outs = []
    for h in range(ATT_HEADS):
        rows = slice(h * tq, (h + 1) * tq)
        denom = _rows(jnp.sum, l_ref[rows])[:, :ATT_HEAD_DIM]
        outs.append(acc_ref[rows, h * ATT_HEAD_DIM:(h + 1) * ATT_HEAD_DIM] / denom)
    o_ref[...] = jnp.concatenate(outs, axis=1)


def _sample_attention(qi, wi, qa, ki_new, ka_new, va_new, cache_idx, cache_k, cache_v, page_table, rel_bias,
                      *, tk=1024, nbis=BISECTIONS):
    s, tq, _ = qi.shape
    n_pages = page_table.shape[1]
    past = n_pages * PAGE_SIZE
    ppt = tk // PAGE_SIZE
    ntiles = past // tk + 1
    keep = min(TOPK_MAX, (past + tq) // 4)
    n_near = _n_near(tk)
    bias = _bias_tiles(rel_bias, tq, tk, n_near).reshape(n_near + 1, ATT_HEADS * tq, tk)
    utri = jnp.asarray(np.triu(np.ones((tk, tk), np.float32)), BF16)

    def per_seq(r, w):
        return pl.BlockSpec((None, r, w), lambda i, pt: (i, 0, 0))

    def full(a):
        return _resident(a.shape, lambda i, pt: (0,) * a.ndim)

    hbm = pl.BlockSpec(memory_space=pl.ANY)
    n_slots = min(SAMPLE_KV_SLOTS, past // tk)
    kern = functools.partial(_sattn_kernel, n_pages=n_pages, n_slots=n_slots, tq=tq, tk=tk, keep=keep, nbis=nbis,
                             n_near=n_near)
    grid_spec = pltpu.PrefetchScalarGridSpec(
        num_scalar_prefetch=1,
        grid=(s,),
        in_specs=[per_seq(tq, 512), per_seq(tq, IDX_HEADS), per_seq(tq, 512),
                  per_seq(IDX_DIM, tq), per_seq(ATT_WIDTH, tq), per_seq(ATT_WIDTH, tq),
                  full(bias), full(utri), hbm, hbm, hbm],
        out_specs=per_seq(tq, 512),
        scratch_shapes=[pltpu.VMEM((2, n_pages + ppt, IDX_DIM, PAGE_SIZE), F32),
                        pltpu.VMEM((n_slots, ppt, ATT_WIDTH, PAGE_SIZE), F32),
                        pltpu.VMEM((n_slots, ppt, ATT_WIDTH, PAGE_SIZE), F32),
                        pltpu.VMEM((ATT_WIDTH, tk), F32),
                        pltpu.VMEM((ATT_WIDTH, tk), F32),
                        pltpu.SemaphoreType.DMA((2 + 2 * n_slots,)),
                        pltpu.VMEM((ntiles, tq, tk), F32),
                        pltpu.VMEM((IDX_HEADS * tq, IDX_DIM), BF16),
                        pltpu.VMEM((IDX_HEADS, tq, LANES), F32),
                        pltpu.VMEM((ATT_HEADS * tq, ATT_WIDTH), BF16),
                        pltpu.VMEM((ATT_HEADS * tq, LANES), F32),
                        pltpu.VMEM((ATT_HEADS * tq, LANES), F32),
                        pltpu.VMEM((ATT_HEADS * tq, ATT_WIDTH), F32),
                        pltpu.VMEM((ATT_HEADS * tq, tk), F32),
                        pltpu.VMEM((ATT_HEADS * tq, tk), F32)])
    return pl.pallas_call(
        kern,
        grid_spec=grid_spec,
        out_shape=jax.ShapeDtypeStruct((s, tq, ATT_WIDTH), F32),
        compiler_params=pltpu.CompilerParams(dimension_semantics=("arbitrary",),
                                             vmem_limit_bytes=VMEM_LIMIT_BYTES),
        name="sattn",
    )(page_table, qi, wi, qa, ki_new, ka_new, va_new, bias, utri, cache_idx, cache_k, cache_v)


def _merge_kernel(x_ref, yr_ref, ya_ref, sgr_ref, sga_ref, wr_ref, wa_ref, wo_ref, n2_ref,
                  wfi_ref, wfo_ref, nf_ref, y_ref):
    m = (sgr_ref[...] * jnp.dot(yr_ref[...], wr_ref[...], preferred_element_type=F32)
         + sga_ref[...] * jnp.dot(ya_ref[...], wa_ref[...], preferred_element_type=F32))
    x = x_ref[...] + jnp.dot(m.astype(BF16), wo_ref[...], preferred_element_type=F32)
    h = (x * lax.rsqrt(jnp.mean(x * x, axis=-1, keepdims=True) + NORM_EPS) * n2_ref[...]).astype(BF16)
    u = jnp.dot(h, wfi_ref[:, :FFN_HIDDEN], preferred_element_type=F32)
    gt = jnp.dot(h, wfi_ref[:, FFN_HIDDEN:], preferred_element_type=F32)
    x = x + jnp.dot((jax.nn.silu(gt) * u).astype(BF16), wfo_ref[...], preferred_element_type=F32)
    y_ref[...] = x * lax.rsqrt(jnp.mean(x * x, axis=-1, keepdims=True) + NORM_EPS) * nf_ref[...]


def _merge(x, yr, ya, sgr, sga, wr, wa, wo, n2, wfi, wfo, nf, *, tm):
    rows = x.shape[0]

    def row(w):
        return pl.BlockSpec((tm, w), lambda i: (i, 0))

    def full(a):
        return _resident(a.shape, lambda i: (0,) * a.ndim)

    return pl.pallas_call(
        _merge_kernel,
        grid=(rows // tm,),
        in_specs=[row(D_MODEL), row(1024), row(ATT_WIDTH), row(D_MODEL), row(D_MODEL),
                  full(wr), full(wa), full(wo), full(n2), full(wfi), full(wfo), full(nf)],
        out_specs=row(D_MODEL),
        out_shape=jax.ShapeDtypeStruct((rows, D_MODEL), F32),
        compiler_params=pltpu.CompilerParams(dimension_semantics=("parallel",),
                                             vmem_limit_bytes=VMEM_LIMIT_BYTES),
        name="merge",
    )(x, yr, ya, sgr, sga, wr, wa, wo, n2, wfi, wfo, nf)


_W_QI_END = 5120
_W_KI_END = _W_QI_END + IDX_DIM
_W_WI_END = _W_KI_END + IDX_HEADS


def _prep_weights(w_in, idx_ln_g, idx_ln_b):
    wm = jnp.concatenate([w_in[:, :_W_QI_END], w_in[:, _W_WI_END:]], axis=1).astype(BF16)
    wk = jnp.concatenate([w_in[:, _W_QI_END:_W_KI_END]] * 2, axis=1).astype(BF16)
    ww = jnp.pad(w_in[:, _W_KI_END:_W_WI_END], ((0, 0), (0, LANES - IDX_HEADS))).astype(BF16)
    lng = jnp.concatenate([idx_ln_g] * 2)[None, :]
    lnb = jnp.concatenate([idx_ln_b] * 2)[None, :]
    return wm, wk, ww, lng, lnb


def _rot_tables(pos):
    inv_freq = ROPE_BASE ** (-jnp.arange(0, RET_QK_DIM, 2, dtype=F32) / RET_QK_DIM)
    ang = pos.astype(F32)[:, None] * inv_freq[None, :]
    cos = jnp.cos(ang)
    sin = jnp.sin(ang)
    return jnp.concatenate([cos, cos], axis=1), jnp.concatenate([-sin, sin], axis=1)


def _token_minor_cache(cache_idx, cache_k, cache_v):
    n_pool = cache_k.shape[0]
    return (jnp.transpose(cache_idx, (0, 2, 1)),
            jnp.transpose(cache_k, (0, 2, 3, 1)).reshape(n_pool, ATT_WIDTH, PAGE_SIZE),
            jnp.transpose(cache_v, (0, 2, 3, 1)).reshape(n_pool, ATT_WIDTH, PAGE_SIZE))


def _project_group(x, pos_tile, norm1_g, proj_w, *, tm, act_dtype):
    b, t, _ = x.shape
    cos_t, sin_t = _rot_tables(pos_tile)
    outs = _proj(x.reshape(b * t, D_MODEL), norm1_g[None, :], *proj_w[:3], cos_t, sin_t, *proj_w[3:],
                 tm=tm, act_dtype=act_dtype)
    names = ("qr", "kr", "vr", "gr", "qa", "ka", "va", "kab", "vab", "qi", "ki", "kk", "wi", "sgr", "sga")
    return {n: o.reshape(b, t, o.shape[-1]) for n, o in zip(names, outs)}


def _finish_group(x, p, y_ret, y_att, merge_w, *, tm):
    b, t, _ = x.shape
    flat = lambda a: a.reshape(b * t, a.shape[-1])
    y = _merge(flat(x), flat(y_ret), flat(y_att), flat(p["sgr"]), flat(p["sga"]), *merge_w, tm=tm)
    return y.reshape(b, t, D_MODEL)


def kernel(x_prompt, x_sample, cache_k, cache_v, cache_idx_k, state_ret, page_table, norm1_g, w_in,
           idx_ln_g, idx_ln_b, w_ret_branch, w_att_branch, w_out, norm2_g, w_ffn_in, w_ffn_out,
           rel_bias, norm_f_g):
    depth = w_in.shape[0]
    assert depth == 1
    l = 0
    b, t, _ = x_prompt.shape
    db, dt, _ = x_sample.shape
    past = page_table.shape[1] * PAGE_SIZE
    proj_w = _prep_weights(w_in[l], idx_ln_g[l], idx_ln_b[l])
    merge_w = (w_ret_branch[l].astype(BF16), w_att_branch[l].astype(BF16), w_out[l].astype(BF16),
               norm2_g[l][None, :], w_ffn_in[l].astype(BF16), w_ffn_out[l].astype(BF16), norm_f_g[None, :])

    tm = 256
    pp = _project_group(x_prompt, jnp.arange(t, dtype=jnp.int32), norm1_g[l], proj_w, tm=tm, act_dtype=BF16)
    s0 = jnp.zeros((b, RET_HEADS, RET_QK_DIM, RET_V_DIM), F32)
    yr_p, s_p = _retention(pp["qr"], pp["kr"], pp["vr"], pp["gr"], s0, RET_CHUNK)
    ya_p = _prompt_attention(pp["qi"], pp["wi"], pp["kk"], pp["qa"], pp["kab"], pp["vab"], rel_bias)
    y_prompt = _finish_group(x_prompt, pp, yr_p, ya_p, merge_w, tm=tm)

    tm_s = min(tm, db * dt)
    ps = _project_group(x_sample, past + jnp.arange(tm_s, dtype=jnp.int32) % dt, norm1_g[l], proj_w,
                        tm=tm_s, act_dtype=F32)
    yr_s, s_s = _retention(ps["qr"], ps["kr"], ps["vr"], ps["gr"], state_ret[l],
                           RET_CHUNK if dt % RET_CHUNK == 0 else dt)
    ya_s = _sample_attention(ps["qi"].astype(F32), ps["wi"], ps["qa"].astype(F32),
                             *(jnp.swapaxes(ps[n], 1, 2) for n in ("ki", "ka", "va")),
                             *_token_minor_cache(cache_idx_k[l], cache_k[l], cache_v[l]),
                             page_table, rel_bias)
    y_sample = _finish_group(x_sample, ps, yr_s, ya_s.astype(BF16), merge_w, tm=tm_s)

    heads = lambda a: a.reshape(a.shape[0], a.shape[1], ATT_HEADS, ATT_HEAD_DIM)[None]
    return (y_prompt, y_sample,
            heads(pp["ka"]), heads(pp["va"]), pp["ki"][None], s_p[None],
            heads(ps["ka"]), heads(ps["va"]), ps["ki"][None], s_s[None])
```

```python
import functools
import itertools
import math

import numpy as np
import jax
import jax.numpy as jnp
from jax import lax
from jax.experimental import pallas as pl
from jax.experimental.pallas import tpu as pltpu

F32 = jnp.float32
BF16 = jnp.bfloat16

D_MODEL = 1024
PAGE_SIZE = 128
RET_HEADS = 4
RET_QK_DIM = 128
RET_V_DIM = 256
RET_CHUNK = 128
ROPE_BASE = 10000.0
ATT_HEADS = 8
ATT_HEAD_DIM = 64
ATT_WIDTH = ATT_HEADS * ATT_HEAD_DIM
IDX_HEADS = 8
IDX_DIM = 64
TOPK_MAX = 256
REL_BUCKETS = 32
REL_MAX_EXACT = 16
REL_MAX_DIST = 128
REL_SATURATE = 128
REL_FIRST_SATURATED = 113
FFN_HIDDEN = 2816
NORM_EPS = 1e-6
LN_EPS = 1e-5

LANES = 128
VMEM_LIMIT_BYTES = 56 * 1024 * 1024
SAMPLE_KV_SLOTS = 6
BISECTIONS = 16

NEG_BIG = -1e30
LOG2E = math.log2(math.e)
F32_MAX = float(np.finfo(np.float32).max)

NT_DIMS = (((1,), (1,)), ((), ()))


def _resident(shape, index_map):
    return pl.BlockSpec(shape, index_map, pipeline_mode=pl.Buffered(1))


_C_QR, _C_KR, _C_VR, _C_GR = 0, 512, 1024, 2048
_C_QA, _C_KA, _C_VA, _C_QI = 3072, 3584, 4096, 4608
_C_SGR, _C_SGA, _C_END = 5120, 6144, 7168


def _proj_kernel(x_ref, g_ref, wm_ref, wk_ref, ww_ref, cos_ref, sin_ref, lng_ref, lnb_ref,
                 qr_ref, kr_ref, vr_ref, gr_ref, qa_ref, ka_ref, va_ref, kab_ref, vab_ref,
                 qi_ref, ki_ref, kk_ref, wi_ref, sgr_ref, sga_ref):
    x = x_ref[...]
    h = (x * lax.rsqrt(jnp.mean(x * x, axis=-1, keepdims=True) + NORM_EPS) * g_ref[...]).astype(BF16)

    def mm(a, b):
        return jnp.dot(h, wm_ref[:, a:b], preferred_element_type=F32)

    cos = cos_ref[...]
    sin = sin_ref[...]
    q = mm(_C_QR, _C_KR)
    k = mm(_C_KR, _C_VR)
    for hd in range(RET_HEADS):
        sl = slice(hd * RET_QK_DIM, (hd + 1) * RET_QK_DIM)
        qh = q[:, sl]
        kh = k[:, sl]
        qr_ref[:, sl] = (qh * cos + pltpu.roll(qh, RET_QK_DIM // 2, 1) * sin).astype(qr_ref.dtype)
        kr_ref[:, sl] = ((kh * cos + pltpu.roll(kh, RET_QK_DIM // 2, 1) * sin)
                         * (RET_QK_DIM ** -0.5)).astype(kr_ref.dtype)
    vr_ref[...] = mm(_C_VR, _C_GR).astype(vr_ref.dtype)
    gr_ref[...] = mm(_C_GR, _C_QA)
    qa_ref[...] = (mm(_C_QA, _C_KA) * (ATT_HEAD_DIM ** -0.5 * LOG2E)).astype(BF16)
    ka = mm(_C_KA, _C_VA)
    ka_ref[...] = ka
    kab_ref[...] = ka.astype(BF16)
    va = mm(_C_VA, _C_QI)
    va_ref[...] = va
    vab_ref[...] = va.astype(BF16)
    qi_ref[...] = (mm(_C_QI, _C_SGR) * (IDX_DIM ** -0.5)).astype(BF16)
    sgr_ref[...] = jax.nn.sigmoid(mm(_C_SGR, _C_SGA))
    sga_ref[...] = jax.nn.sigmoid(mm(_C_SGA, _C_END))

    kd = jnp.dot(h, wk_ref[...], preferred_element_type=F32)
    mu = jnp.mean(kd, axis=-1, keepdims=True)
    var = jnp.mean(jnp.square(kd - mu), axis=-1, keepdims=True)
    kn = (kd - mu) * lax.rsqrt(var + LN_EPS) * lng_ref[...] + lnb_ref[...]
    ki_ref[...] = kn[:, :IDX_DIM]
    kk_ref[...] = kn.astype(BF16)
    wi = jnp.dot(h, ww_ref[...], preferred_element_type=F32)
    wi_ref[...] = wi[:, :IDX_HEADS] * (IDX_HEADS ** -0.5)


def _proj(x, g, wm, wk, ww, cos_t, sin_t, lng, lnb, *, tm, act_dtype):
    rows = x.shape[0]
    nrot = cos_t.shape[0] // tm
    grid = (rows // tm,)

    def row(w):
        return pl.BlockSpec((tm, w), lambda i: (i, 0))

    def full(a):
        return _resident(a.shape, lambda i: (0,) * a.ndim)

    rot = pl.BlockSpec((tm, LANES), lambda i: (i % nrot, 0))
    widths = [(512, act_dtype), (512, act_dtype), (1024, act_dtype), (1024, F32),
              (512, BF16), (512, F32), (512, F32), (512, BF16), (512, BF16),
              (512, BF16), (IDX_DIM, F32), (LANES, BF16), (IDX_HEADS, F32),
              (1024, F32), (1024, F32)]
    return pl.pallas_call(
        _proj_kernel,
        grid=grid,
        in_specs=[row(D_MODEL), full(g), full(wm), full(wk), full(ww), rot, rot, full(lng), full(lnb)],
        out_specs=[row(w) for w, _ in widths],
        out_shape=[jax.ShapeDtypeStruct((rows, w), dt) for w, dt in widths],
        compiler_params=pltpu.CompilerParams(dimension_semantics=("parallel",),
                                             vmem_limit_bytes=VMEM_LIMIT_BYTES),
        name="proj",
    )(x, g, wm, wk, ww, cos_t, sin_t, lng, lnb)


def _ret_kernel(q_ref, k_ref, v_ref, g_ref, s0_ref, dm_ref, qd_ref, kd_ref, cd_ref,
                y_ref, s_ref, *, mxu_dtype):
    @pl.when(pl.program_id(1) == 0)
    def _():
        s_ref[...] = s0_ref[...]

    for seq, h in itertools.product(range(q_ref.shape[0]), range(RET_HEADS)):
        qk = slice(h * RET_QK_DIM, (h + 1) * RET_QK_DIM)
        vv = slice(h * RET_V_DIM, (h + 1) * RET_V_DIM)
        q = q_ref[seq, :, qk].astype(F32)
        k = k_ref[seq, :, qk].astype(F32)
        v = v_ref[seq, :, vv].astype(mxu_dtype)
        state = s_ref[seq, h]
        sc = lax.dot_general(q.astype(mxu_dtype), k.astype(mxu_dtype), NT_DIMS,
                             preferred_element_type=F32) * dm_ref[h]
        o = (jnp.dot(sc.astype(mxu_dtype), v, preferred_element_type=F32)
             + jnp.dot((q * qd_ref[h]).astype(mxu_dtype), state.astype(mxu_dtype),
                       preferred_element_type=F32))
        kdec = (k * kd_ref[h]).T.astype(mxu_dtype)
        s_ref[seq, h] = state * cd_ref[h] + jnp.dot(kdec, v, preferred_element_type=F32)
        mu = jnp.mean(o, axis=-1, keepdims=True)
        var = jnp.mean(jnp.square(o - mu), axis=-1, keepdims=True)
        on = (o - mu) * lax.rsqrt(var + LN_EPS)
        y_ref[seq, :, vv] = (jax.nn.silu(g_ref[seq, :, vv]) * on).astype(y_ref.dtype)


def _retention(q, k, v, g, s0, chunk, group=1):
    b, t, _ = q.shape
    n = t // chunk
    log_g = np.log1p(-np.exp2(-5.0 - np.arange(RET_HEADS, dtype=np.float64)))
    pos = np.arange(chunk, dtype=np.float64)
    diff = pos[:, None] - pos[None, :]
    dmask = np.where(diff >= 0, np.exp(log_g[:, None, None] * np.maximum(diff, 0.0)), 0.0)
    q_dec = np.broadcast_to(np.exp(log_g[:, None] * (pos + 1.0))[..., None], (RET_HEADS, chunk, RET_QK_DIM))
    k_dec = np.broadcast_to(np.exp(log_g[:, None] * (chunk - 1.0 - pos))[..., None], (RET_HEADS, chunk, RET_QK_DIM))
    c_dec = np.broadcast_to(np.exp(log_g * chunk)[:, None, None], (RET_HEADS, 1, RET_V_DIM))
    consts = [jnp.asarray(a, F32) for a in (dmask, q_dec, k_dec, c_dec)]
    mxu_dtype = BF16 if chunk % 16 == 0 else F32

    def tok(w):
        return pl.BlockSpec((group, chunk, w), lambda i, j: (i, j, 0))

    def const(a):
        return _resident(a.shape, lambda i, j: (0,) * a.ndim)

    state = pl.BlockSpec((group, RET_HEADS, RET_QK_DIM, RET_V_DIM), lambda i, j: (i, 0, 0, 0))
    return pl.pallas_call(
        functools.partial(_ret_kernel, mxu_dtype=mxu_dtype),
        grid=(b // group, n),
        in_specs=[tok(512), tok(512), tok(1024), tok(1024), state] + [const(a) for a in consts],
        out_specs=[tok(1024), state],
        out_shape=[jax.ShapeDtypeStruct((b, t, 1024), BF16),
                   jax.ShapeDtypeStruct((b, RET_HEADS, RET_QK_DIM, RET_V_DIM), F32)],
        compiler_params=pltpu.CompilerParams(dimension_semantics=("parallel", "arbitrary"),
                                             vmem_limit_bytes=VMEM_LIMIT_BYTES),
        name="retention",
    )(q, k, v, g, s0, *consts)


def _rel_bucket_np(dist):
    dist = np.asarray(dist)
    nf = np.maximum(dist, 1).astype(np.float64)
    large = REL_MAX_EXACT + (np.log(nf / REL_MAX_EXACT) / math.log(REL_MAX_DIST / REL_MAX_EXACT)
                             * (REL_BUCKETS - REL_MAX_EXACT)).astype(np.int32)
    return np.where(dist < REL_MAX_EXACT, dist, np.minimum(large, REL_BUCKETS - 1))


def _fold(op, a):
    parts = [a[:, c * LANES:(c + 1) * LANES] for c in range(a.shape[1] // LANES)]
    return functools.reduce(op, parts)


def _rows(op, a):
    return jnp.broadcast_to(op(a, axis=-1, keepdims=True), (a.shape[0], LANES))


def _rep(a, width):
    return a if width == LANES else jnp.concatenate([a] * (width // LANES), axis=1)


def _tile_loop(ntiles, body, init):
    if not isinstance(ntiles, int):
        return lax.fori_loop(0, ntiles, body, init)
    carry = init
    for t in range(ntiles):
        carry = body(t, carry)
    return carry


def _kth_largest(tile, ntiles, lo, hi, small, keep, tk, nbis):
    kf = float(keep)
    rows = lo.shape[0]
    zero = jnp.zeros((rows, LANES), F32)

    def count_ge(thr):
        thr_t = _rep(thr, tk)

        def body(t, c):
            return c + _fold(jnp.add, jnp.where(tile(t) >= thr_t, 1.0, 0.0))

        return _rows(jnp.sum, _tile_loop(ntiles, body, zero))

    def bisect(_, carry):
        lo, hi = carry
        mid = 0.5 * lo + 0.5 * hi
        ge = count_ge(mid) >= kf
        return jnp.where(ge, mid, lo), jnp.where(ge, hi, mid)

    def quadrisect(_, carry):
        lo, hi = carry
        cuts = [(1.0 - f) * lo + f * hi for f in (0.25, 0.5, 0.75)]
        cuts_t = [_rep(c, tk) for c in cuts]

        def body(t, counts):
            st = tile(t)
            return tuple(c + _fold(jnp.add, jnp.where(st >= ct, 1.0, 0.0)) for c, ct in zip(counts, cuts_t))

        ge = [_rows(jnp.sum, c) >= kf for c in _tile_loop(ntiles, body, (zero, zero, zero))]
        new_lo = jnp.where(ge[2], cuts[2], jnp.where(ge[1], cuts[1], jnp.where(ge[0], cuts[0], lo)))
        new_hi = jnp.where(ge[2], hi, jnp.where(ge[1], cuts[2], jnp.where(ge[0], cuts[1], cuts[0])))
        return new_lo, new_hi

    if isinstance(ntiles, int) and rows <= 8:
        lo, _ = lax.fori_loop(0, nbis // 2, quadrisect, (lo, hi))
    else:
        lo, _ = lax.fori_loop(0, nbis, bisect, (lo, hi))

    def snap_body(t, c):
        st = tile(t)
        return jnp.minimum(c, _fold(jnp.minimum, jnp.where(st >= _rep(lo, tk), st, jnp.inf)))

    val = _rows(jnp.min, _tile_loop(ntiles, snap_body, jnp.full((rows, LANES), jnp.inf, F32)))

    def stats(val):
        val_t = _rep(val, tk)

        def body(t, c):
            cge, cgt, nxt = c
            st = tile(t)
            gt = st > val_t
            cge = cge + _fold(jnp.add, jnp.where(st >= val_t, 1.0, 0.0))
            cgt = cgt + _fold(jnp.add, jnp.where(gt, 1.0, 0.0))
            nxt = jnp.minimum(nxt, _fold(jnp.minimum, jnp.where(gt, st, jnp.inf)))
            return cge, cgt, nxt

        cge, cgt, nxt = _tile_loop(ntiles, body, (zero, zero, jnp.full((rows, LANES), jnp.inf, F32)))
        return _rows(jnp.sum, cge), _rows(jnp.sum, cgt), _rows(jnp.min, nxt)

    def pending(cgt):
        return jnp.sum(jnp.where((cgt >= kf) & jnp.logical_not(small), 1.0, 0.0))

    def cond(c):
        return c[4] > 0.0

    def step(c):
        val, _, cgt, nxt, _ = c
        val = jnp.where((cgt >= kf) & jnp.logical_not(small), nxt, val)
        cge, cgt, nxt = stats(val)
        return val, cge, cgt, nxt, pending(cgt)

    cge, cgt, nxt = stats(val)
    val, cge, cgt, _, _ = lax.while_loop(cond, step, (val, cge, cgt, nxt, pending(cgt)))

    tau = jnp.where(small, -F32_MAX, val)
    quota = jnp.where(small, F32_MAX, kf - cgt)
    need_fix = jnp.sum(jnp.where((cge > kf) & jnp.logical_not(small), 1.0, 0.0))
    return tau, quota, need_fix


def _drop_surplus_ties(read, write, ntiles, tau, quota, utri, tk):
    def body(t, run):
        st = read(t)
        tie = st == _rep(tau, tk)
        prefix = jnp.dot(jnp.where(tie, 1.0, 0.0).astype(BF16), utri, preferred_element_type=F32)
        over = (_rep(run, tk) + prefix) > _rep(quota, tk)
        write(t, jnp.where(tie, jnp.where(over, -jnp.inf, st), st))
        return run + _rows(jnp.sum, jnp.where(tie, 1.0, 0.0))

    lax.fori_loop(0, ntiles, body, jnp.zeros(tau.shape, F32))


def _sparse_attend(q0, idx_keys, att_keys, att_vals, qi_ref, wi_ref, qa_ref, bias_ref, utri_ref, o_ref,
                   s_ref, qim_ref, wb_ref, qam_ref, m_ref, l_ref, acc_ref, lga_ref, lgb_ref, xa_ref, xb_ref,
                   *, tq, tk, keep, nbis, n_near):
    ntiles = (q0 + tq + tk - 1) // tk
    lane = lax.broadcasted_iota(jnp.int32, (tq, LANES), 1)
    lo_half = lane < (LANES // 2)

    qi = qi_ref[...].astype(F32)
    qim_ref[...] = jnp.concatenate([qi[:, h * IDX_DIM:(h + 1) * IDX_DIM] for h in range(IDX_HEADS)],
                                   axis=0).astype(BF16)
    for h in range(IDX_HEADS):
        wb_ref[h] = jnp.broadcast_to(wi_ref[:, h:h + 1], (tq, LANES))
    for j in range(ATT_HEADS // 2):
        qp = qa_ref[:, j * LANES:(j + 1) * LANES].astype(F32)
        qam_ref[j] = jnp.concatenate([jnp.where(lo_half, qp, 0.0), jnp.where(lo_half, 0.0, qp)],
                                     axis=0).astype(BF16)

    qpos = q0 + lax.broadcasted_iota(jnp.int32, (tq, tk), 0)
    kiota = lax.broadcasted_iota(jnp.int32, (tq, tk), 1)

    def head_products(t, dst_ref):
        dst_ref[...] = lax.dot_general(qim_ref[...], idx_keys(jnp.minimum(t, ntiles - 1)), NT_DIMS,
                                       preferred_element_type=F32)

    def reduce_heads(t, x_ref, carry):
        mx, mn = carry
        tc = jnp.minimum(t, ntiles - 1)
        s = jnp.zeros((tq, tk), F32)
        for h in range(IDX_HEADS):
            s = s + _rep(wb_ref[h], tk) * jnp.maximum(x_ref[h * tq:(h + 1) * tq, :], 0.0)
        s_ref[tc] = jnp.where(tc * tk + kiota <= qpos, s, -jnp.inf)
        return jnp.maximum(mx, _fold(jnp.maximum, s)), jnp.minimum(mn, _fold(jnp.minimum, s))

    def score_two_tiles(i, carry):
        t = 2 * i
        head_products(t + 1, xb_ref)
        carry = reduce_heads(t, xa_ref, carry)
        head_products(t + 2, xa_ref)
        return reduce_heads(t + 1, xb_ref, carry)

    head_products(0, xa_ref)
    mx, mn = lax.fori_loop(0, (ntiles + 1) // 2, score_two_tiles,
                           (jnp.full((tq, LANES), -jnp.inf, F32), jnp.full((tq, LANES), jnp.inf, F32)))
    hi = _rows(jnp.max, mx)
    lo = _rows(jnp.min, mn)

    n_visible = q0 + lax.broadcasted_iota(jnp.int32, (tq, LANES), 0) + 1
    small = n_visible <= keep
    tau, quota, need_fix = _kth_largest(lambda t: s_ref[t], ntiles, lo, hi, small, keep, tk, nbis)

    @pl.when(need_fix > 0.0)
    def _():
        def write(t, val):
            s_ref[t] = val
        _drop_surplus_ties(lambda t: s_ref[t], write, ntiles, tau, quota, utri_ref[...], tk)

    n_far = jnp.clip((q0 - LANES * n_near + tk) // tk, 0, ntiles)

    def masked_logits(t, stop, near, dst_ref):
        tc = jnp.minimum(t, ntiles - 1)
        thr = _rep(jnp.where(t < stop, tau, jnp.inf), tk)
        sel = s_ref[tc] >= thr
        for j in range(ATT_HEADS // 2):
            logits = lax.dot_general(qam_ref[j], att_keys(tc, j), NT_DIMS, preferred_element_type=F32)
            if near:
                logits = logits + bias_ref[jnp.minimum((q0 - tc * tk) // LANES, n_near), j]
            dst_ref[j] = jnp.concatenate([jnp.where(sel, logits[:tq], NEG_BIG),
                                          jnp.where(sel, logits[tq:], NEG_BIG)], axis=0)

    def sweep(start, stop, near, consume):
        masked_logits(start, stop, near, lga_ref)

        def body(i, carry):
            t = start + 2 * i
            masked_logits(t + 1, stop, near, lgb_ref)
            for j in range(ATT_HEADS // 2):
                consume(t, j, near, lga_ref[j])
            masked_logits(t + 2, stop, near, lga_ref)
            for j in range(ATT_HEADS // 2):
                consume(t + 1, j, near, lgb_ref[j])
            return carry

        lax.fori_loop(0, (stop - start + 1) // 2, body, 0)

    def take_online(t, j, near, lm):
        m_old = m_ref[j]
        row_max = _rows(jnp.max, _fold(jnp.maximum, lm))
        if near:
            m_new = jnp.maximum(m_old, row_max)
            shift = m_new
        else:
            far_bias = bias_ref[n_near, j, :, 0:LANES]
            m_new = jnp.maximum(m_old, row_max + far_bias)
            shift = m_new - far_bias
        alpha = jnp.exp2(m_old - m_new)
        p = jnp.exp2(lm - _rep(shift, tk))
        l_ref[j] = alpha * l_ref[j] + _fold(jnp.add, p)
        m_ref[j] = m_new
        acc_ref[j] = alpha * acc_ref[j] + jnp.dot(p.astype(BF16), att_vals(jnp.minimum(t, ntiles - 1), j),
                                                  preferred_element_type=F32)

    m_ref[...] = jnp.full(m_ref.shape, NEG_BIG, F32)
    l_ref[...] = jnp.zeros(l_ref.shape, F32)
    acc_ref[...] = jnp.zeros(acc_ref.shape, F32)
    sweep(0, n_far, False, take_online)
    sweep(n_far, ntiles, True, take_online)

    for j in range(ATT_HEADS // 2):
        out = acc_ref[j] / _rows(jnp.sum, l_ref[j])
        o_ref[:, j * LANES:(j + 1) * LANES] = jnp.where(lo_half, out[:tq], out[tq:]).astype(o_ref.dtype)


def _bias_tiles(rel_bias, tq, tk, n_near):
    table = rel_bias[_rel_bucket_np(np.arange(REL_SATURATE + 1))].T * LOG2E
    period = tq + tk
    m = np.arange(period)
    q_minus_k = np.where(m < tk, -m, period - m)
    tiles = []
    for v in range(n_near):
        strip = table[:, np.clip(LANES * v + q_minus_k, 0, REL_SATURATE)]
        skew = jnp.tile(strip, (1, tq))[:, :tq * (period - 1)].reshape(ATT_HEADS, tq, period - 1)
        tiles.append(skew[:, :, :tk])
    tiles.append(jnp.broadcast_to(table[:, REL_SATURATE][:, None, None], (ATT_HEADS, tq, tk)))
    return jnp.stack(tiles).astype(F32)


def _n_near(tk):
    return (tk - 1 + REL_FIRST_SATURATED - 1) // LANES + 1


def _attend_scratch(tq, tk, ntiles):
    return [pltpu.VMEM((ntiles, tq, tk), F32),
            pltpu.VMEM((IDX_HEADS * tq, IDX_DIM), BF16),
            pltpu.VMEM((IDX_HEADS, tq, LANES), F32),
            pltpu.VMEM((ATT_HEADS // 2, 2 * tq, LANES), BF16)] + [
            pltpu.VMEM((ATT_HEADS // 2, 2 * tq, LANES), F32)] * 3 + [
            pltpu.VMEM((ATT_HEADS // 2, 2 * tq, tk), F32)] * 2 + [
            pltpu.VMEM((IDX_HEADS * tq, tk), F32)] * 2


def _pattn_kernel(qi_ref, wi_ref, kk_ref, qa_ref, ka_ref, va_ref, bias_ref, utri_ref, o_ref, *scratch,
                  tq, tk, **static):
    def rows(t):
        return pl.ds(pl.multiple_of(t * tk, tk), tk)

    _sparse_attend(pl.program_id(1) * tq,
                   lambda t: kk_ref[rows(t), :IDX_DIM],
                   lambda t, j: ka_ref[rows(t), j * LANES:(j + 1) * LANES],
                   lambda t, j: va_ref[rows(t), j * LANES:(j + 1) * LANES],
                   qi_ref, wi_ref, qa_ref, bias_ref, utri_ref, o_ref, *scratch, tq=tq, tk=tk, **static)


def _prompt_attention(qi, wi, kk, qa, ka, va, rel_bias, *, tq=128, tk=512, nbis=BISECTIONS):
    b, t, _ = qi.shape
    tk = min(tk, t)
    keep = min(TOPK_MAX, max(t // 4, 1))
    n_near = _n_near(tk)
    bias = _bias_tiles(rel_bias, tq, tk, n_near).reshape(n_near + 1, ATT_HEADS // 2, 2 * tq, tk)
    utri = jnp.asarray(np.triu(np.ones((tk, tk), np.float32)), BF16)

    def qblk(w):
        return pl.BlockSpec((None, tq, w), lambda i, j: (i, j, 0))

    def seq(w):
        return _resident((None, t, w), lambda i, j: (i, 0, 0))

    def full(a):
        return _resident(a.shape, lambda i, j: (0,) * a.ndim)

    kern = functools.partial(_pattn_kernel, tq=tq, tk=tk, keep=keep, nbis=nbis, n_near=n_near)
    return pl.pallas_call(
        kern,
        grid=(b, t // tq),
        in_specs=[qblk(512), qblk(IDX_HEADS), seq(LANES), qblk(512), seq(512), seq(512), full(bias), full(utri)],
        out_specs=qblk(512),
        out_shape=jax.ShapeDtypeStruct((b, t, ATT_WIDTH), BF16),
        scratch_shapes=_attend_scratch(tq, tk, t // tk),
        compiler_params=pltpu.CompilerParams(dimension_semantics=("parallel", "arbitrary"),
                                             vmem_limit_bytes=VMEM_LIMIT_BYTES),
        name="pattn",
    )(qi, wi, kk, qa, ka, va, bias, utri)


def _sattn_kernel(pt_ref, qi_ref, wi_ref, qa_ref, kin_ref, kan_ref, van_ref, bias_ref, utri_ref,
                  cidx_hbm, ck_hbm, cv_hbm, o_ref,
                  idx_buf, k_buf, v_buf, knew_buf, vnew_buf, sem,
                  s_ref, qim_ref, wb_ref, qbd_ref, m_ref, l_ref, acc_ref, lga_ref, lgb_ref,
                  *, n_pages, n_slots, tq, tk, keep, nbis, n_near):
    seq = pl.program_id(0)
    n_seq = pl.num_programs(0)
    past = n_pages * PAGE_SIZE
    ppt = tk // PAGE_SIZE
    n_cached = past // tk
    ntiles = n_cached + 1
    ibuf = seq % 2

    def idx_copy(s, p):
        return pltpu.make_async_copy(cidx_hbm.at[pt_ref[s, p]], idx_buf.at[s % 2, p], sem.at[s % 2])

    def tile_copies(s, t):
        slot = (s * n_cached + t) % n_slots
        copies = []
        for i in range(ppt):
            page = pt_ref[s, t * ppt + i]
            copies.append(pltpu.make_async_copy(ck_hbm.at[page], k_buf.at[slot, i], sem.at[2 + slot]))
            copies.append(pltpu.make_async_copy(cv_hbm.at[page], v_buf.at[slot, i], sem.at[2 + n_slots + slot]))
        return copies

    def start_idx(s):
        def body(p, carry):
            idx_copy(s, p).start()
            return carry
        lax.fori_loop(0, n_pages, body, 0)

    def start_tile(s, t):
        for c in tile_copies(s, t):
            c.start()

    @pl.when(seq == 0)
    def _():
        start_idx(0)
        for t in range(n_slots):
            start_tile(0, t)
        idx_buf[:, n_pages:n_pages + ppt] = jnp.zeros((2, ppt, IDX_DIM, PAGE_SIZE), F32)
        knew_buf[...] = jnp.zeros(knew_buf.shape, F32)
        vnew_buf[...] = jnp.zeros(vnew_buf.shape, F32)

    idx_buf[ibuf, n_pages, :, 0:tq] = kin_ref[...]
    knew_buf[:, 0:tq] = kan_ref[...]
    vnew_buf[:, 0:tq] = van_ref[...]

    qi = qi_ref[...]
    qim_ref[...] = jnp.concatenate([qi[:, h * IDX_DIM:(h + 1) * IDX_DIM] for h in range(IDX_HEADS)],
                                   axis=0).astype(BF16)
    for h in range(IDX_HEADS):
        wb_ref[h] = jnp.broadcast_to(wi_ref[:, h:h + 1], (tq, LANES))
    head_of_lane = lax.broadcasted_iota(jnp.int32, (tq, ATT_WIDTH), 1) // ATT_HEAD_DIM
    qa = qa_ref[...]
    qbd_ref[...] = jnp.concatenate([jnp.where(head_of_lane == h, qa, 0.0) for h in range(ATT_HEADS)],
                                   axis=0).astype(BF16)

    def wait_idx(p, carry):
        idx_copy(seq, p).wait()
        return carry

    lax.fori_loop(0, n_pages, wait_idx, 0)

    @pl.when(seq + 1 < n_seq)
    def _():
        start_idx(seq + 1)

    def lanes_of(pages):
        return jnp.concatenate([pages[i] for i in range(ppt)], axis=1)

    lg_refs = (lga_ref, lgb_ref)

    def head_products(t):
        keys_t = lanes_of(idx_buf[ibuf, t * ppt:(t + 1) * ppt])
        lg_refs[t % 2][...] = jnp.dot(qim_ref[...], keys_t.astype(BF16), preferred_element_type=F32)

    new_visible = (lax.broadcasted_iota(jnp.int32, (tq, tk), 1)
                   <= lax.broadcasted_iota(jnp.int32, (tq, tk), 0))
    mx = jnp.full((tq, LANES), -jnp.inf, F32)
    mn = jnp.full((tq, LANES), jnp.inf, F32)
    head_products(0)
    for t in range(ntiles):
        if t + 1 < ntiles:
            head_products(t + 1)
        s = jnp.zeros((tq, tk), F32)
        for h in range(IDX_HEADS):
            s = s + _rep(wb_ref[h], tk) * jnp.maximum(lg_refs[t % 2][h * tq:(h + 1) * tq, :], 0.0)
        s_ref[t] = jnp.where(new_visible, s, -jnp.inf) if t == n_cached else s
        mx = jnp.maximum(mx, _fold(jnp.maximum, s))
        mn = jnp.minimum(mn, _fold(jnp.minimum, s))
    hi = _rows(jnp.max, mx)
    lo = _rows(jnp.min, mn)

    small = jnp.zeros((tq, LANES), jnp.bool_)
    tau, quota, need_fix = _kth_largest(lambda t: s_ref[t], ntiles, lo, hi, small, keep, tk, nbis)

    @pl.when(need_fix > 0.0)
    def _():
        def write(t, val):
            s_ref[t] = val
        _drop_surplus_ties(lambda t: s_ref[t], write, ntiles, tau, quota, utri_ref[...], tk)

    m_ref[...] = jnp.full(m_ref.shape, NEG_BIG, F32)
    l_ref[...] = jnp.zeros(l_ref.shape, F32)
    acc_ref[...] = jnp.zeros(acc_ref.shape, F32)
    tau_t = _rep(tau, tk)

    variants = [min((past - t * tk) // LANES, n_near) for t in range(ntiles)]

    def slot_of(t):
        return (seq * n_cached + t) % n_slots

    def masked_logits(t):
        if t < n_cached:
            for c in tile_copies(seq, t):
                c.wait()
            keys_t = lanes_of(k_buf[slot_of(t)])
        else:
            keys_t = knew_buf[...]
        hide = jnp.where(s_ref[t] >= tau_t, 0.0, NEG_BIG)
        lm = (jnp.dot(qbd_ref[...], keys_t.astype(BF16), preferred_element_type=F32)
              + jnp.concatenate([hide] * ATT_HEADS, axis=0))
        if variants[t] < n_near:
            lm = lm + bias_ref[variants[t]]
        lg_refs[t % 2][...] = lm

    def consume(t):
        lm = lg_refs[t % 2][...]
        m_old = m_ref[...]
        row_max = _rows(jnp.max, _fold(jnp.maximum, lm))
        if variants[t] < n_near:
            m_new = jnp.maximum(m_old, row_max)
            shift = m_new
        else:
            far_bias = bias_ref[n_near, :, 0:LANES]
            m_new = jnp.maximum(m_old, row_max + far_bias)
            shift = m_new - far_bias
        alpha = jnp.exp2(m_old - m_new)
        p = jnp.exp2(lm - _rep(shift, tk))
        l_ref[...] = alpha * l_ref[...] + _fold(jnp.add, p)
        m_ref[...] = m_new
        vals_t = lanes_of(v_buf[slot_of(t)]) if t < n_cached else vnew_buf[...]
        pv = lax.dot_general(p.astype(BF16), vals_t.astype(BF16), NT_DIMS, preferred_element_type=F32)
        acc_ref[...] = _rep(alpha, ATT_WIDTH) * acc_ref[...] + pv

    def refill(t):
        ahead = t + n_slots
        nxt_seq = seq + ahead // n_cached

        @pl.when(nxt_seq < n_seq)
        def _():
            start_tile(nxt_seq, ahead % n_cached)

    masked_logits(0)
    for t in range(ntiles):
        if t + 1 < ntiles:
            masked_logits(t + 1)
        consume(t)
        if t < n_cached:
            refill(t)

    outs = []
    for h in range(ATT_HEADS):
        rows = slice(h * tq, (h + 1) * tq)
        denom = _rows(jnp.sum, l_ref[rows])[:, :ATT_HEAD_DIM]
        outs.append(acc_ref[rows, h * ATT_HEAD_DIM:(h + 1) * ATT_HEAD_DIM] / denom)
    o_ref[...] = jnp.concatenate(outs, axis=1)


def _sample_attention(qi, wi, qa, ki_new, ka_new, va_new, cache_idx, cache_k, cache_v, page_table, rel_bias,
                      *, tk=1024, nbis=BISECTIONS):
    s, tq, _ = qi.shape
    n_pages = page_table.shape[1]
    past = n_pages * PAGE_SIZE
    ppt = tk // PAGE_SIZE
    ntiles = past // tk + 1
    keep = min(TOPK_MAX, (past + tq) // 4)
    n_near = _n_near(tk)
    bias = _bias_tiles(rel_bias, tq, tk, n_near).reshape(n_near + 1, ATT_HEADS * tq, tk)
    utri = jnp.asarray(np.triu(np.ones((tk, tk), np.float32)), BF16)

    def per_seq(r, w):
        return pl.BlockSpec((None, r, w), lambda i, pt: (i, 0, 0))

    def full(a):
        return _resident(a.shape, lambda i, pt: (0,) * a.ndim)

    hbm = pl.BlockSpec(memory_space=pl.ANY)
    n_slots = min(SAMPLE_KV_SLOTS, past // tk)
    kern = functools.partial(_sattn_kernel, n_pages=n_pages, n_slots=n_slots, tq=tq, tk=tk, keep=keep, nbis=nbis,
                             n_near=n_near)
    grid_spec = pltpu.PrefetchScalarGridSpec(
        num_scalar_prefetch=1,
        grid=(s,),
        in_specs=[per_seq(tq, 512), per_seq(tq, IDX_HEADS), per_seq(tq, 512),
                  per_seq(IDX_DIM, tq), per_seq(ATT_WIDTH, tq), per_seq(ATT_WIDTH, tq),
                  full(bias), full(utri), hbm, hbm, hbm],
        out_specs=per_seq(tq, 512),
        scratch_shapes=[pltpu.VMEM((2, n_pages + ppt, IDX_DIM, PAGE_SIZE), F32),
                        pltpu.VMEM((n_slots, ppt, ATT_WIDTH, PAGE_SIZE), F32),
                        pltpu.VMEM((n_slots, ppt, ATT_WIDTH, PAGE_SIZE), F32),
                        pltpu.VMEM((ATT_WIDTH, tk), F32),
                        pltpu.VMEM((ATT_WIDTH, tk), F32),
                        pltpu.SemaphoreType.DMA((2 + 2 * n_slots,)),
                        pltpu.VMEM((ntiles, tq, tk), F32),
                        pltpu.VMEM((IDX_HEADS * tq, IDX_DIM), BF16),
                        pltpu.VMEM((IDX_HEADS, tq, LANES), F32),
                        pltpu.VMEM((ATT_HEADS * tq, ATT_WIDTH), BF16),
                        pltpu.VMEM((ATT_HEADS * tq, LANES), F32),
                        pltpu.VMEM((ATT_HEADS * tq, LANES), F32),
                        pltpu.VMEM((ATT_HEADS * tq, ATT_WIDTH), F32),
                        pltpu.VMEM((ATT_HEADS * tq, tk), F32),
                        pltpu.VMEM((ATT_HEADS * tq, tk), F32)])
    return pl.pallas_call(
        kern,
        grid_spec=grid_spec,
        out_shape=jax.ShapeDtypeStruct((s, tq, ATT_WIDTH), F32),
        compiler_params=pltpu.CompilerParams(dimension_semantics=("arbitrary",),
                                             vmem_limit_bytes=VMEM_LIMIT_BYTES),
        name="sattn",
    )(page_table, qi, wi, qa, ki_new, ka_new, va_new, bias, utri, cache_idx, cache_k, cache_v)


def _merge_kernel(x_ref, yr_ref, ya_ref, sgr_ref, sga_ref, wr_ref, wa_ref, wo_ref, n2_ref,
                  wfi_ref, wfo_ref, nf_ref, y_ref):
    m = (sgr_ref[...] * jnp.dot(yr_ref[...], wr_ref[...], preferred_element_type=F32)
         + sga_ref[...] * jnp.dot(ya_ref[...], wa_ref[...], preferred_element_type=F32))
    x = x_ref[...] + jnp.dot(m.astype(BF16), wo_ref[...], preferred_element_type=F32)
    h = (x * lax.rsqrt(jnp.mean(x * x, axis=-1, keepdims=True) + NORM_EPS) * n2_ref[...]).astype(BF16)
    u = jnp.dot(h, wfi_ref[:, :FFN_HIDDEN], preferred_element_type=F32)
    gt = jnp.dot(h, wfi_ref[:, FFN_HIDDEN:], preferred_element_type=F32)
    x = x + jnp.dot((jax.nn.silu(gt) * u).astype(BF16), wfo_ref[...], preferred_element_type=F32)
    y_ref[...] = x * lax.rsqrt(jnp.mean(x * x, axis=-1, keepdims=True) + NORM_EPS) * nf_ref[...]


def _merge(x, yr, ya, sgr, sga, wr, wa, wo, n2, wfi, wfo, nf, *, tm):
    rows = x.shape[0]

    def row(w):
        return pl.BlockSpec((tm, w), lambda i: (i, 0))

    def full(a):
        return _resident(a.shape, lambda i: (0,) * a.ndim)

    return pl.pallas_call(
        _merge_kernel,
        grid=(rows // tm,),
        in_specs=[row(D_MODEL), row(1024), row(ATT_WIDTH), row(D_MODEL), row(D_MODEL),
                  full(wr), full(wa), full(wo), full(n2), full(wfi), full(wfo), full(nf)],
        out_specs=row(D_MODEL),
        out_shape=jax.ShapeDtypeStruct((rows, D_MODEL), F32),
        compiler_params=pltpu.CompilerParams(dimension_semantics=("parallel",),
                                             vmem_limit_bytes=VMEM_LIMIT_BYTES),
        name="merge",
    )(x, yr, ya, sgr, sga, wr, wa, wo, n2, wfi, wfo, nf)


_W_QI_END = 5120
_W_KI_END = _W_QI_END + IDX_DIM
_W_WI_END = _W_KI_END + IDX_HEADS


def _prep_weights(w_in, idx_ln_g, idx_ln_b):
    wm = jnp.concatenate([w_in[:, :_W_QI_END], w_in[:, _W_WI_END:]], axis=1).astype(BF16)
    wk = jnp.concatenate([w_in[:, _W_QI_END:_W_KI_END]] * 2, axis=1).astype(BF16)
    ww = jnp.pad(w_in[:, _W_KI_END:_W_WI_END], ((0, 0), (0, LANES - IDX_HEADS))).astype(BF16)
    lng = jnp.concatenate([idx_ln_g] * 2)[None, :]
    lnb = jnp.concatenate([idx_ln_b] * 2)[None, :]
    return wm, wk, ww, lng, lnb


def _rot_tables(pos):
    inv_freq = ROPE_BASE ** (-jnp.arange(0, RET_QK_DIM, 2, dtype=F32) / RET_QK_DIM)
    ang = pos.astype(F32)[:, None] * inv_freq[None, :]
    cos = jnp.cos(ang)
    sin = jnp.sin(ang)
    return jnp.concatenate([cos, cos], axis=1), jnp.concatenate([-sin, sin], axis=1)


def _token_minor_cache(cache_idx, cache_k, cache_v):
    n_pool = cache_k.shape[0]
    return (jnp.transpose(cache_idx, (0, 2, 1)),
            jnp.transpose(cache_k, (0, 2, 3, 1)).reshape(n_pool, ATT_WIDTH, PAGE_SIZE),
            jnp.transpose(cache_v, (0, 2, 3, 1)).reshape(n_pool, ATT_WIDTH, PAGE_SIZE))


def _project_group(x, pos_tile, norm1_g, proj_w, *, tm, act_dtype):
    b, t, _ = x.shape
    cos_t, sin_t = _rot_tables(pos_tile)
    outs = _proj(x.reshape(b * t, D_MODEL), norm1_g[None, :], *proj_w[:3], cos_t, sin_t, *proj_w[3:],
                 tm=tm, act_dtype=act_dtype)
    names = ("qr", "kr", "vr", "gr", "qa", "ka", "va", "kab", "vab", "qi", "ki", "kk", "wi", "sgr", "sga")
    return {n: o.reshape(b, t, o.shape[-1]) for n, o in zip(names, outs)}


def _finish_group(x, p, y_ret, y_att, merge_w, *, tm):
    b, t, _ = x.shape
    flat = lambda a: a.reshape(b * t, a.shape[-1])
    y = _merge(flat(x), flat(y_ret), flat(y_att), flat(p["sgr"]), flat(p["sga"]), *merge_w, tm=tm)
    return y.reshape(b, t, D_MODEL)


def kernel(x_prompt, x_sample, cache_k, cache_v, cache_idx_k, state_ret, page_table, norm1_g, w_in,
           idx_ln_g, idx_ln_b, w_ret_branch, w_att_branch, w_out, norm2_g, w_ffn_in, w_ffn_out,
           rel_bias, norm_f_g):
    depth = w_in.shape[0]
    assert depth == 1
    l = 0
    b, t, _ = x_prompt.shape
    db, dt, _ = x_sample.shape
    past = page_table.shape[1] * PAGE_SIZE
    proj_w = _prep_weights(w_in[l], idx_ln_g[l], idx_ln_b[l])
    merge_w = (w_ret_branch[l].astype(BF16), w_att_branch[l].astype(BF16), w_out[l].astype(BF16),
               norm2_g[l][None, :], w_ffn_in[l].astype(BF16), w_ffn_out[l].astype(BF16), norm_f_g[None, :])

    tm = 256
    pp = _project_group(x_prompt, jnp.arange(t, dtype=jnp.int32), norm1_g[l], proj_w, tm=tm, act_dtype=BF16)
    s0 = jnp.zeros((b, RET_HEADS, RET_QK_DIM, RET_V_DIM), F32)
    yr_p, s_p = _retention(pp["qr"], pp["kr"], pp["vr"], pp["gr"], s0, RET_CHUNK)
    ya_p = _prompt_attention(pp["qi"], pp["wi"], pp["kk"], pp["qa"], pp["kab"], pp["vab"], rel_bias)
    y_prompt = _finish_group(x_prompt, pp, yr_p, ya_p, merge_w, tm=tm)

    tm_s = min(tm, db * dt)
    ps = _project_group(x_sample, past + jnp.arange(tm_s, dtype=jnp.int32) % dt, norm1_g[l], proj_w,
                        tm=tm_s, act_dtype=F32)
    yr_s, s_s = _retention(ps["qr"], ps["kr"], ps["vr"], ps["gr"], state_ret[l],
                           RET_CHUNK if dt % RET_CHUNK == 0 else dt, group=2 if db % 2 == 0 else 1)
    ya_s = _sample_attention(ps["qi"].astype(F32), ps["wi"], ps["qa"].astype(F32),
                             *(jnp.swapaxes(ps[n], 1, 2) for n in ("ki", "ka", "va")),
                             *_token_minor_cache(cache_idx_k[l], cache_k[l], cache_v[l]),
                             page_table, rel_bias)
    y_sample = _finish_group(x_sample, ps, yr_s, ya_s.astype(BF16), merge_w, tm=tm_s)

    heads = lambda a: a.reshape(a.shape[0], a.shape[1], ATT_HEADS, ATT_HEAD_DIM)[None]
    return (y_prompt, y_sample,
            heads(pp["ka"]), heads(pp["va"]), pp["ki"][None], s_p[None],
            heads(ps["ka"]), heads(ps["va"]), ps["ki"][None], s_s[None])
```

```python
import functools
import itertools
import math

import numpy as np
import jax
import jax.numpy as jnp
from jax import lax
from jax.experimental import pallas as pl
from jax.experimental.pallas import tpu as pltpu

F32 = jnp.float32
BF16 = jnp.bfloat16

D_MODEL = 1024
PAGE_SIZE = 128
RET_HEADS = 4
RET_QK_DIM = 128
RET_V_DIM = 256
RET_CHUNK = 128
ROPE_BASE = 10000.0
ATT_HEADS = 8
ATT_HEAD_DIM = 64
ATT_WIDTH = ATT_HEADS * ATT_HEAD_DIM
IDX_HEADS = 8
IDX_DIM = 64
TOPK_MAX = 256
REL_BUCKETS = 32
REL_MAX_EXACT = 16
REL_MAX_DIST = 128
REL_SATURATE = 128
REL_FIRST_SATURATED = 113
FFN_HIDDEN = 2816
NORM_EPS = 1e-6
LN_EPS = 1e-5

LANES = 128
VMEM_LIMIT_BYTES = 56 * 1024 * 1024
SAMPLE_KV_SLOTS = 6
BISECTIONS = 16

NEG_BIG = -1e30
LOG2E = math.log2(math.e)
F32_MAX = float(np.finfo(np.float32).max)

NT_DIMS = (((1,), (1,)), ((), ()))


def _resident(shape, index_map):
    return pl.BlockSpec(shape, index_map, pipeline_mode=pl.Buffered(1))


_C_QR, _C_KR, _C_VR, _C_GR = 0, 512, 1024, 2048
_C_QA, _C_KA, _C_VA, _C_QI = 3072, 3584, 4096, 4608
_C_SGR, _C_SGA, _C_END = 5120, 6144, 7168


def _proj_kernel(x_ref, g_ref, wm_ref, wk_ref, ww_ref, cos_ref, sin_ref, lng_ref, lnb_ref,
                 qr_ref, kr_ref, vr_ref, gr_ref, qa_ref, ka_ref, va_ref, kab_ref, vab_ref,
                 qi_ref, ki_ref, kk_ref, wi_ref, sgr_ref, sga_ref):
    x = x_ref[...]
    h = (x * lax.rsqrt(jnp.mean(x * x, axis=-1, keepdims=True) + NORM_EPS) * g_ref[...]).astype(BF16)

    def mm(a, b):
        return jnp.dot(h, wm_ref[:, a:b], preferred_element_type=F32)

    cos = cos_ref[...]
    sin = sin_ref[...]
    q = mm(_C_QR, _C_KR)
    k = mm(_C_KR, _C_VR)
    for hd in range(RET_HEADS):
        sl = slice(hd * RET_QK_DIM, (hd + 1) * RET_QK_DIM)
        qh = q[:, sl]
        kh = k[:, sl]
        qr_ref[:, sl] = (qh * cos + pltpu.roll(qh, RET_QK_DIM // 2, 1) * sin).astype(qr_ref.dtype)
        kr_ref[:, sl] = ((kh * cos + pltpu.roll(kh, RET_QK_DIM // 2, 1) * sin)
                         * (RET_QK_DIM ** -0.5)).astype(kr_ref.dtype)
    vr_ref[...] = mm(_C_VR, _C_GR).astype(vr_ref.dtype)
    gr_ref[...] = mm(_C_GR, _C_QA)
    qa_ref[...] = (mm(_C_QA, _C_KA) * (ATT_HEAD_DIM ** -0.5 * LOG2E)).astype(BF16)
    ka = mm(_C_KA, _C_VA)
    ka_ref[...] = ka
    kab_ref[...] = ka.astype(BF16)
    va = mm(_C_VA, _C_QI)
    va_ref[...] = va
    vab_ref[...] = va.astype(BF16)
    qi_ref[...] = (mm(_C_QI, _C_SGR) * (IDX_DIM ** -0.5)).astype(BF16)
    sgr_ref[...] = jax.nn.sigmoid(mm(_C_SGR, _C_SGA))
    sga_ref[...] = jax.nn.sigmoid(mm(_C_SGA, _C_END))

    kd = jnp.dot(h, wk_ref[...], preferred_element_type=F32)
    mu = jnp.mean(kd, axis=-1, keepdims=True)
    var = jnp.mean(jnp.square(kd - mu), axis=-1, keepdims=True)
    kn = (kd - mu) * lax.rsqrt(var + LN_EPS) * lng_ref[...] + lnb_ref[...]
    ki_ref[...] = kn[:, :IDX_DIM]
    kk_ref[...] = kn.astype(BF16)
    wi = jnp.dot(h, ww_ref[...], preferred_element_type=F32)
    wi_ref[...] = wi[:, :IDX_HEADS] * (IDX_HEADS ** -0.5)


def _proj(x, g, wm, wk, ww, cos_t, sin_t, lng, lnb, *, tm, act_dtype):
    rows = x.shape[0]
    nrot = cos_t.shape[0] // tm
    grid = (rows // tm,)

    def row(w):
        return pl.BlockSpec((tm, w), lambda i: (i, 0))

    def full(a):
        return _resident(a.shape, lambda i: (0,) * a.ndim)

    rot = pl.BlockSpec((tm, LANES), lambda i: (i % nrot, 0))
    widths = [(512, act_dtype), (512, act_dtype), (1024, act_dtype), (1024, F32),
              (512, BF16), (512, F32), (512, F32), (512, BF16), (512, BF16),
              (512, BF16), (IDX_DIM, F32), (LANES, BF16), (IDX_HEADS, F32),
              (1024, F32), (1024, F32)]
    return pl.pallas_call(
        _proj_kernel,
        grid=grid,
        in_specs=[row(D_MODEL), full(g), full(wm), full(wk), full(ww), rot, rot, full(lng), full(lnb)],
        out_specs=[row(w) for w, _ in widths],
        out_shape=[jax.ShapeDtypeStruct((rows, w), dt) for w, dt in widths],
        compiler_params=pltpu.CompilerParams(dimension_semantics=("parallel",),
                                             vmem_limit_bytes=VMEM_LIMIT_BYTES),
        name="proj",
    )(x, g, wm, wk, ww, cos_t, sin_t, lng, lnb)


def _ret_kernel(q_ref, k_ref, v_ref, g_ref, s0_ref, dm_ref, qd_ref, kd_ref, cd_ref,
                y_ref, s_ref, *, mxu_dtype):
    @pl.when(pl.program_id(1) == 0)
    def _():
        s_ref[...] = s0_ref[...]

    for seq, h in itertools.product(range(q_ref.shape[0]), range(RET_HEADS)):
        qk = slice(h * RET_QK_DIM, (h + 1) * RET_QK_DIM)
        vv = slice(h * RET_V_DIM, (h + 1) * RET_V_DIM)
        q = q_ref[seq, :, qk].astype(F32)
        k = k_ref[seq, :, qk].astype(F32)
        v = v_ref[seq, :, vv].astype(mxu_dtype)
        state = s_ref[seq, h]
        sc = lax.dot_general(q.astype(mxu_dtype), k.astype(mxu_dtype), NT_DIMS,
                             preferred_element_type=F32) * dm_ref[h]
        o = (jnp.dot(sc.astype(mxu_dtype), v, preferred_element_type=F32)
             + jnp.dot((q * qd_ref[h]).astype(mxu_dtype), state.astype(mxu_dtype),
                       preferred_element_type=F32))
        kdec = (k * kd_ref[h]).T.astype(mxu_dtype)
        s_ref[seq, h] = state * cd_ref[h] + jnp.dot(kdec, v, preferred_element_type=F32)
        mu = jnp.mean(o, axis=-1, keepdims=True)
        var = jnp.mean(jnp.square(o - mu), axis=-1, keepdims=True)
        on = (o - mu) * lax.rsqrt(var + LN_EPS)
        y_ref[seq, :, vv] = (jax.nn.silu(g_ref[seq, :, vv]) * on).astype(y_ref.dtype)


def _retention(q, k, v, g, s0, chunk, group=1):
    b, t, _ = q.shape
    n = t // chunk
    log_g = np.log1p(-np.exp2(-5.0 - np.arange(RET_HEADS, dtype=np.float64)))
    pos = np.arange(chunk, dtype=np.float64)
    diff = pos[:, None] - pos[None, :]
    dmask = np.where(diff >= 0, np.exp(log_g[:, None, None] * np.maximum(diff, 0.0)), 0.0)
    q_dec = np.broadcast_to(np.exp(log_g[:, None] * (pos + 1.0))[..., None], (RET_HEADS, chunk, RET_QK_DIM))
    k_dec = np.broadcast_to(np.exp(log_g[:, None] * (chunk - 1.0 - pos))[..., None], (RET_HEADS, chunk, RET_QK_DIM))
    c_dec = np.broadcast_to(np.exp(log_g * chunk)[:, None, None], (RET_HEADS, 1, RET_V_DIM))
    consts = [jnp.asarray(a, F32) for a in (dmask, q_dec, k_dec, c_dec)]
    mxu_dtype = BF16 if chunk % 16 == 0 else F32

    def tok(w):
        return pl.BlockSpec((group, chunk, w), lambda i, j: (i, j, 0))

    def const(a):
        return _resident(a.shape, lambda i, j: (0,) * a.ndim)

    state = pl.BlockSpec((group, RET_HEADS, RET_QK_DIM, RET_V_DIM), lambda i, j: (i, 0, 0, 0))
    return pl.pallas_call(
        functools.partial(_ret_kernel, mxu_dtype=mxu_dtype),
        grid=(b // group, n),
        in_specs=[tok(512), tok(512), tok(1024), tok(1024), state] + [const(a) for a in consts],
        out_specs=[tok(1024), state],
        out_shape=[jax.ShapeDtypeStruct((b, t, 1024), BF16),
                   jax.ShapeDtypeStruct((b, RET_HEADS, RET_QK_DIM, RET_V_DIM), F32)],
        compiler_params=pltpu.CompilerParams(dimension_semantics=("parallel", "arbitrary"),
                                             vmem_limit_bytes=VMEM_LIMIT_BYTES),
        name="retention",
    )(q, k, v, g, s0, *consts)


def _rel_bucket_np(dist):
    dist = np.asarray(dist)
    nf = np.maximum(dist, 1).astype(np.float64)
    large = REL_MAX_EXACT + (np.log(nf / REL_MAX_EXACT) / math.log(REL_MAX_DIST / REL_MAX_EXACT)
                             * (REL_BUCKETS - REL_MAX_EXACT)).astype(np.int32)
    return np.where(dist < REL_MAX_EXACT, dist, np.minimum(large, REL_BUCKETS - 1))


def _fold(op, a):
    parts = [a[:, c * LANES:(c + 1) * LANES] for c in range(a.shape[1] // LANES)]
    return functools.reduce(op, parts)


def _rows(op, a):
    return jnp.broadcast_to(op(a, axis=-1, keepdims=True), (a.shape[0], LANES))


def _rep(a, width):
    return a if width == LANES else jnp.concatenate([a] * (width // LANES), axis=1)


def _tile_loop(ntiles, body, init):
    if not isinstance(ntiles, int):
        return lax.fori_loop(0, ntiles, body, init)
    carry = init
    for t in range(ntiles):
        carry = body(t, carry)
    return carry


def _kth_largest(tile, ntiles, lo, hi, count_lo, small, keep, tk, nbis):
    kf = float(keep)
    rows = lo.shape[0]
    zero = jnp.zeros((rows, LANES), F32)

    def count_ge(thr):
        thr_t = _rep(thr, tk)

        def body(t, c):
            return c + _fold(jnp.add, jnp.where(tile(t) >= thr_t, 1.0, 0.0))

        return _rows(jnp.sum, _tile_loop(ntiles, body, zero))

    def bisect(_, carry):
        lo, hi, c_lo = carry
        mid = 0.5 * lo + 0.5 * hi
        c_mid = count_ge(mid)
        ge = c_mid >= kf
        return jnp.where(ge, mid, lo), jnp.where(ge, hi, mid), jnp.where(ge, c_mid, c_lo)

    def quadrisect(_, carry):
        lo, hi, c_lo = carry
        cuts = [(1.0 - f) * lo + f * hi for f in (0.25, 0.5, 0.75)]
        cuts_t = [_rep(c, tk) for c in cuts]

        def body(t, counts):
            st = tile(t)
            return tuple(c + _fold(jnp.add, jnp.where(st >= ct, 1.0, 0.0)) for c, ct in zip(counts, cuts_t))

        cnt = [_rows(jnp.sum, c) for c in _tile_loop(ntiles, body, (zero, zero, zero))]
        ge = [c >= kf for c in cnt]
        new_lo = jnp.where(ge[2], cuts[2], jnp.where(ge[1], cuts[1], jnp.where(ge[0], cuts[0], lo)))
        new_hi = jnp.where(ge[2], hi, jnp.where(ge[1], cuts[2], jnp.where(ge[0], cuts[1], cuts[0])))
        new_c = jnp.where(ge[2], cnt[2], jnp.where(ge[1], cnt[1], jnp.where(ge[0], cnt[0], c_lo)))
        return new_lo, new_hi, new_c

    if isinstance(ntiles, int) and rows <= 8:
        lo, _, cge = lax.fori_loop(0, nbis // 2, quadrisect, (lo, hi, count_lo))
    else:
        lo, _, cge = lax.fori_loop(0, nbis, bisect, (lo, hi, count_lo))

    def snap_body(t, c):
        st = tile(t)
        return jnp.minimum(c, _fold(jnp.minimum, jnp.where(st >= _rep(lo, tk), st, jnp.inf)))

    val = _rows(jnp.min, _tile_loop(ntiles, snap_body, jnp.full((rows, LANES), jnp.inf, F32)))

    def above(val):
        val_t = _rep(val, tk)

        def body(t, c):
            cgt, nxt = c
            st = tile(t)
            gt = st > val_t
            cgt = cgt + _fold(jnp.add, jnp.where(gt, 1.0, 0.0))
            nxt = jnp.minimum(nxt, _fold(jnp.minimum, jnp.where(gt, st, jnp.inf)))
            return cgt, nxt

        cgt, nxt = _tile_loop(ntiles, body, (zero, jnp.full((rows, LANES), jnp.inf, F32)))
        return _rows(jnp.sum, cgt), _rows(jnp.min, nxt)

    def pending(cgt):
        return jnp.sum(jnp.where((cgt >= kf) & jnp.logical_not(small), 1.0, 0.0))

    def cond(c):
        return c[4] > 0.0

    def step(c):
        val, cge, cgt, nxt, _ = c
        advance = (cgt >= kf) & jnp.logical_not(small)
        val = jnp.where(advance, nxt, val)
        cge = jnp.where(advance, cgt, cge)
        cgt, nxt = above(val)
        return val, cge, cgt, nxt, pending(cgt)

    cgt, nxt = above(val)
    val, cge, cgt, _, _ = lax.while_loop(cond, step, (val, cge, cgt, nxt, pending(cgt)))

    tau = jnp.where(small, -F32_MAX, val)
    quota = jnp.where(small, F32_MAX, kf - cgt)
    need_fix = jnp.sum(jnp.where((cge > kf) & jnp.logical_not(small), 1.0, 0.0))
    return tau, quota, need_fix


def _drop_surplus_ties(read, write, ntiles, tau, quota, utri, tk):
    def body(t, run):
        st = read(t)
        tie = st == _rep(tau, tk)
        prefix = jnp.dot(jnp.where(tie, 1.0, 0.0).astype(BF16), utri, preferred_element_type=F32)
        over = (_rep(run, tk) + prefix) > _rep(quota, tk)
        write(t, jnp.where(tie, jnp.where(over, -jnp.inf, st), st))
        return run + _rows(jnp.sum, jnp.where(tie, 1.0, 0.0))

    lax.fori_loop(0, ntiles, body, jnp.zeros(tau.shape, F32))


def _sparse_attend(q0, idx_keys, att_keys, att_vals, qi_ref, wi_ref, qa_ref, bias_ref, utri_ref, o_ref,
                   s_ref, qim_ref, wb_ref, qam_ref, m_ref, l_ref, acc_ref, lga_ref, lgb_ref, xa_ref, xb_ref,
                   *, tq, tk, keep, nbis, n_near):
    ntiles = (q0 + tq + tk - 1) // tk
    lane = lax.broadcasted_iota(jnp.int32, (tq, LANES), 1)
    lo_half = lane < (LANES // 2)

    qi = qi_ref[...].astype(F32)
    qim_ref[...] = jnp.concatenate([qi[:, h * IDX_DIM:(h + 1) * IDX_DIM] for h in range(IDX_HEADS)],
                                   axis=0).astype(BF16)
    for h in range(IDX_HEADS):
        wb_ref[h] = jnp.broadcast_to(wi_ref[:, h:h + 1], (tq, LANES))
    for j in range(ATT_HEADS // 2):
        qp = qa_ref[:, j * LANES:(j + 1) * LANES].astype(F32)
        qam_ref[j] = jnp.concatenate([jnp.where(lo_half, qp, 0.0), jnp.where(lo_half, 0.0, qp)],
                                     axis=0).astype(BF16)

    qpos = q0 + lax.broadcasted_iota(jnp.int32, (tq, tk), 0)
    kiota = lax.broadcasted_iota(jnp.int32, (tq, tk), 1)

    def head_products(t, dst_ref):
        dst_ref[...] = lax.dot_general(qim_ref[...], idx_keys(jnp.minimum(t, ntiles - 1)), NT_DIMS,
                                       preferred_element_type=F32)

    def reduce_heads(t, x_ref, carry):
        mx, mn = carry
        tc = jnp.minimum(t, ntiles - 1)
        s = jnp.zeros((tq, tk), F32)
        for h in range(IDX_HEADS):
            s = s + _rep(wb_ref[h], tk) * jnp.maximum(x_ref[h * tq:(h + 1) * tq, :], 0.0)
        s_ref[tc] = jnp.where(tc * tk + kiota <= qpos, s, -jnp.inf)
        return jnp.maximum(mx, _fold(jnp.maximum, s)), jnp.minimum(mn, _fold(jnp.minimum, s))

    def score_two_tiles(i, carry):
        t = 2 * i
        head_products(t + 1, xb_ref)
        carry = reduce_heads(t, xa_ref, carry)
        head_products(t + 2, xa_ref)
        return reduce_heads(t + 1, xb_ref, carry)

    head_products(0, xa_ref)
    mx, mn = lax.fori_loop(0, (ntiles + 1) // 2, score_two_tiles,
                           (jnp.full((tq, LANES), -jnp.inf, F32), jnp.full((tq, LANES), jnp.inf, F32)))
    hi = _rows(jnp.max, mx)
    lo = _rows(jnp.min, mn)

    n_visible = q0 + lax.broadcasted_iota(jnp.int32, (tq, LANES), 0) + 1
    small = n_visible <= keep
    tau, quota, need_fix = _kth_largest(lambda t: s_ref[t], ntiles, lo, hi, n_visible.astype(F32), small,
                                        keep, tk, nbis)

    @pl.when(need_fix > 0.0)
    def _():
        def write(t, val):
            s_ref[t] = val
        _drop_surplus_ties(lambda t: s_ref[t], write, ntiles, tau, quota, utri_ref[...], tk)

    n_far = jnp.clip((q0 - LANES * n_near + tk) // tk, 0, ntiles)

    def masked_logits(t, stop, near, dst_ref):
        tc = jnp.minimum(t, ntiles - 1)
        thr = _rep(jnp.where(t < stop, tau, jnp.inf), tk)
        sel = s_ref[tc] >= thr
        for j in range(ATT_HEADS // 2):
            logits = lax.dot_general(qam_ref[j], att_keys(tc, j), NT_DIMS, preferred_element_type=F32)
            if near:
                logits = logits + bias_ref[jnp.minimum((q0 - tc * tk) // LANES, n_near), j]
            dst_ref[j] = jnp.concatenate([jnp.where(sel, logits[:tq], NEG_BIG),
                                          jnp.where(sel, logits[tq:], NEG_BIG)], axis=0)

    def sweep(start, stop, near, consume):
        masked_logits(start, stop, near, lga_ref)

        def body(i, carry):
            t = start + 2 * i
            masked_logits(t + 1, stop, near, lgb_ref)
            for j in range(ATT_HEADS // 2):
                consume(t, j, near, lga_ref[j])
            masked_logits(t + 2, stop, near, lga_ref)
            for j in range(ATT_HEADS // 2):
                consume(t + 1, j, near, lgb_ref[j])
            return carry

        lax.fori_loop(0, (stop - start + 1) // 2, body, 0)

    def take_online(t, j, near, lm):
        m_old = m_ref[j]
        row_max = _rows(jnp.max, _fold(jnp.maximum, lm))
        if near:
            m_new = jnp.maximum(m_old, row_max)
            shift = m_new
        else:
            far_bias = bias_ref[n_near, j, :, 0:LANES]
            m_new = jnp.maximum(m_old, row_max + far_bias)
            shift = m_new - far_bias
        alpha = jnp.exp2(m_old - m_new)
        p = jnp.exp2(lm - _rep(shift, tk))
        l_ref[j] = alpha * l_ref[j] + _fold(jnp.add, p)
        m_ref[j] = m_new
        acc_ref[j] = alpha * acc_ref[j] + jnp.dot(p.astype(BF16), att_vals(jnp.minimum(t, ntiles - 1), j),
                                                  preferred_element_type=F32)

    m_ref[...] = jnp.full(m_ref.shape, NEG_BIG, F32)
    l_ref[...] = jnp.zeros(l_ref.shape, F32)
    acc_ref[...] = jnp.zeros(acc_ref.shape, F32)
    sweep(0, n_far, False, take_online)
    sweep(n_far, ntiles, True, take_online)

    for j in range(ATT_HEADS // 2):
        out = acc_ref[j] / _rows(jnp.sum, l_ref[j])
        o_ref[:, j * LANES:(j + 1) * LANES] = jnp.where(lo_half, out[:tq], out[tq:]).astype(o_ref.dtype)


def _bias_tiles(rel_bias, tq, tk, n_near):
    table = rel_bias[_rel_bucket_np(np.arange(REL_SATURATE + 1))].T * LOG2E
    period = tq + tk
    m = np.arange(period)
    q_minus_k = np.where(m < tk, -m, period - m)
    tiles = []
    for v in range(n_near):
        strip = table[:, np.clip(LANES * v + q_minus_k, 0, REL_SATURATE)]
        skew = jnp.tile(strip, (1, tq))[:, :tq * (period - 1)].reshape(ATT_HEADS, tq, period - 1)
        tiles.append(skew[:, :, :tk])
    tiles.append(jnp.broadcast_to(table[:, REL_SATURATE][:, None, None], (ATT_HEADS, tq, tk)))
    return jnp.stack(tiles).astype(F32)


def _n_near(tk):
    return (tk - 1 + REL_FIRST_SATURATED - 1) // LANES + 1


def _attend_scratch(tq, tk, ntiles):
    return [pltpu.VMEM((ntiles, tq, tk), F32),
            pltpu.VMEM((IDX_HEADS * tq, IDX_DIM), BF16),
            pltpu.VMEM((IDX_HEADS, tq, LANES), F32),
            pltpu.VMEM((ATT_HEADS // 2, 2 * tq, LANES), BF16)] + [
            pltpu.VMEM((ATT_HEADS // 2, 2 * tq, LANES), F32)] * 3 + [
            pltpu.VMEM((ATT_HEADS // 2, 2 * tq, tk), F32)] * 2 + [
            pltpu.VMEM((IDX_HEADS * tq, tk), F32)] * 2


def _pattn_kernel(qi_ref, wi_ref, kk_ref, qa_ref, ka_ref, va_ref, bias_ref, utri_ref, o_ref, *scratch,
                  tq, tk, **static):
    def rows(t):
        return pl.ds(pl.multiple_of(t * tk, tk), tk)

    _sparse_attend(pl.program_id(1) * tq,
                   lambda t: kk_ref[rows(t), :IDX_DIM],
                   lambda t, j: ka_ref[rows(t), j * LANES:(j + 1) * LANES],
                   lambda t, j: va_ref[rows(t), j * LANES:(j + 1) * LANES],
                   qi_ref, wi_ref, qa_ref, bias_ref, utri_ref, o_ref, *scratch, tq=tq, tk=tk, **static)


def _prompt_attention(qi, wi, kk, qa, ka, va, rel_bias, *, tq=128, tk=512, nbis=BISECTIONS):
    b, t, _ = qi.shape
    tk = min(tk, t)
    keep = min(TOPK_MAX, max(t // 4, 1))
    n_near = _n_near(tk)
    bias = _bias_tiles(rel_bias, tq, tk, n_near).reshape(n_near + 1, ATT_HEADS // 2, 2 * tq, tk)
    utri = jnp.asarray(np.triu(np.ones((tk, tk), np.float32)), BF16)

    def qblk(w):
        return pl.BlockSpec((None, tq, w), lambda i, j: (i, j, 0))

    def seq(w):
        return _resident((None, t, w), lambda i, j: (i, 0, 0))

    def full(a):
        return _resident(a.shape, lambda i, j: (0,) * a.ndim)

    kern = functools.partial(_pattn_kernel, tq=tq, tk=tk, keep=keep, nbis=nbis, n_near=n_near)
    return pl.pallas_call(
        kern,
        grid=(b, t // tq),
        in_specs=[qblk(512), qblk(IDX_HEADS), seq(LANES), qblk(512), seq(512), seq(512), full(bias), full(utri)],
        out_specs=qblk(512),
        out_shape=jax.ShapeDtypeStruct((b, t, ATT_WIDTH), BF16),
        scratch_shapes=_attend_scratch(tq, tk, t // tk),
        compiler_params=pltpu.CompilerParams(dimension_semantics=("parallel", "arbitrary"),
                                             vmem_limit_bytes=VMEM_LIMIT_BYTES),
        name="pattn",
    )(qi, wi, kk, qa, ka, va, bias, utri)


def _sattn_kernel(pt_ref, qi_ref, wi_ref, qa_ref, kin_ref, kan_ref, van_ref, bias_ref, utri_ref,
                  cidx_hbm, ck_hbm, cv_hbm, o_ref,
                  idx_buf, k_buf, v_buf, knew_buf, vnew_buf, sem,
                  s_ref, qim_ref, wb_ref, qbd_ref, m_ref, l_ref, acc_ref, lga_ref, lgb_ref,
                  *, n_pages, n_slots, tq, tk, keep, nbis, n_near):
    seq = pl.program_id(0)
    n_seq = pl.num_programs(0)
    past = n_pages * PAGE_SIZE
    ppt = tk // PAGE_SIZE
    n_cached = past // tk
    ntiles = n_cached + 1
    ibuf = seq % 2

    def idx_copy(s, p):
        return pltpu.make_async_copy(cidx_hbm.at[pt_ref[s, p]], idx_buf.at[s % 2, p], sem.at[s % 2])

    def tile_copies(s, t):
        slot = (s * n_cached + t) % n_slots
        copies = []
        for i in range(ppt):
            page = pt_ref[s, t * ppt + i]
            copies.append(pltpu.make_async_copy(ck_hbm.at[page], k_buf.at[slot, i], sem.at[2 + slot]))
            copies.append(pltpu.make_async_copy(cv_hbm.at[page], v_buf.at[slot, i], sem.at[2 + n_slots + slot]))
        return copies

    def start_idx(s):
        def body(p, carry):
            idx_copy(s, p).start()
            return carry
        lax.fori_loop(0, n_pages, body, 0)

    def start_tile(s, t):
        for c in tile_copies(s, t):
            c.start()

    @pl.when(seq == 0)
    def _():
        start_idx(0)
        for t in range(n_slots):
            start_tile(0, t)
        idx_buf[:, n_pages:n_pages + ppt] = jnp.zeros((2, ppt, IDX_DIM, PAGE_SIZE), F32)
        knew_buf[...] = jnp.zeros(knew_buf.shape, F32)
        vnew_buf[...] = jnp.zeros(vnew_buf.shape, F32)

    idx_buf[ibuf, n_pages, :, 0:tq] = kin_ref[...]
    knew_buf[:, 0:tq] = kan_ref[...]
    vnew_buf[:, 0:tq] = van_ref[...]

    qi = qi_ref[...]
    qim_ref[...] = jnp.concatenate([qi[:, h * IDX_DIM:(h + 1) * IDX_DIM] for h in range(IDX_HEADS)],
                                   axis=0).astype(BF16)
    for h in range(IDX_HEADS):
        wb_ref[h] = jnp.broadcast_to(wi_ref[:, h:h + 1], (tq, LANES))
    head_of_lane = lax.broadcasted_iota(jnp.int32, (tq, ATT_WIDTH), 1) // ATT_HEAD_DIM
    qa = qa_ref[...]
    qbd_ref[...] = jnp.concatenate([jnp.where(head_of_lane == h, qa, 0.0) for h in range(ATT_HEADS)],
                                   axis=0).astype(BF16)

    def wait_idx(p, carry):
        idx_copy(seq, p).wait()
        return carry

    lax.fori_loop(0, n_pages, wait_idx, 0)

    @pl.when(seq + 1 < n_seq)
    def _():
        start_idx(seq + 1)

    def lanes_of(pages):
        return jnp.concatenate([pages[i] for i in range(ppt)], axis=1)

    lg_refs = (lga_ref, lgb_ref)

    def head_products(t):
        keys_t = lanes_of(idx_buf[ibuf, t * ppt:(t + 1) * ppt])
        lg_refs[t % 2][...] = jnp.dot(qim_ref[...], keys_t.astype(BF16), preferred_element_type=F32)

    new_visible = (lax.broadcasted_iota(jnp.int32, (tq, tk), 1)
                   <= lax.broadcasted_iota(jnp.int32, (tq, tk), 0))
    mx = jnp.full((tq, LANES), -jnp.inf, F32)
    mn = jnp.full((tq, LANES), jnp.inf, F32)
    head_products(0)
    for t in range(ntiles):
        if t + 1 < ntiles:
            head_products(t + 1)
        s = jnp.zeros((tq, tk), F32)
        for h in range(IDX_HEADS):
            s = s + _rep(wb_ref[h], tk) * jnp.maximum(lg_refs[t % 2][h * tq:(h + 1) * tq, :], 0.0)
        s_ref[t] = jnp.where(new_visible, s, -jnp.inf) if t == n_cached else s
        mx = jnp.maximum(mx, _fold(jnp.maximum, s))
        mn = jnp.minimum(mn, _fold(jnp.minimum, s))
    hi = _rows(jnp.max, mx)
    lo = _rows(jnp.min, mn)

    small = jnp.zeros((tq, LANES), jnp.bool_)
    n_visible = (past + 1 + lax.broadcasted_iota(jnp.int32, (tq, LANES), 0)).astype(F32)
    tau, quota, need_fix = _kth_largest(lambda t: s_ref[t], ntiles, lo, hi, n_visible, small, keep, tk, nbis)

    @pl.when(need_fix > 0.0)
    def _():
        def write(t, val):
            s_ref[t] = val
        _drop_surplus_ties(lambda t: s_ref[t], write, ntiles, tau, quota, utri_ref[...], tk)

    m_ref[...] = jnp.full(m_ref.shape, NEG_BIG, F32)
    l_ref[...] = jnp.zeros(l_ref.shape, F32)
    acc_ref[...] = jnp.zeros(acc_ref.shape, F32)
    tau_t = _rep(tau, tk)

    variants = [min((past - t * tk) // LANES, n_near) for t in range(ntiles)]

    def slot_of(t):
        return (seq * n_cached + t) % n_slots

    def masked_logits(t):
        if t < n_cached:
            for c in tile_copies(seq, t):
                c.wait()
            keys_t = lanes_of(k_buf[slot_of(t)])
        else:
            keys_t = knew_buf[...]
        hide = jnp.where(s_ref[t] >= tau_t, 0.0, NEG_BIG)
        lm = (jnp.dot(qbd_ref[...], keys_t.astype(BF16), preferred_element_type=F32)
              + jnp.concatenate([hide] * ATT_HEADS, axis=0))
        if variants[t] < n_near:
            lm = lm + bias_ref[variants[t]]
        lg_refs[t % 2][...] = lm

    def consume(t):
        lm = lg_refs[t % 2][...]
        m_old = m_ref[...]
        row_max = _rows(jnp.max, _fold(jnp.maximum, lm))
        if variants[t] < n_near:
            m_new = jnp.maximum(m_old, row_max)
            shift = m_new
        else:
            far_bias = bias_ref[n_near, :, 0:LANES]
            m_new = jnp.maximum(m_old, row_max + far_bias)
            shift = m_new - far_bias
        alpha = jnp.exp2(m_old - m_new)
        p = jnp.exp2(lm - _rep(shift, tk))
        l_ref[...] = alpha * l_ref[...] + _fold(jnp.add, p)
        m_ref[...] = m_new
        vals_t = lanes_of(v_buf[slot_of(t)]) if t < n_cached else vnew_buf[...]
        pv = lax.dot_general(p.astype(BF16), vals_t.astype(BF16), NT_DIMS, preferred_element_type=F32)
        acc_ref[...] = _rep(alpha, ATT_WIDTH) * acc_ref[...] + pv

    def refill(t):
        ahead = t + n_slots
        nxt_seq = seq + ahead // n_cached

        @pl.when(nxt_seq < n_seq)
        def _():
            start_tile(nxt_seq, ahead % n_cached)

    masked_logits(0)
    for t in range(ntiles):
        if t + 1 < ntiles:
            masked_logits(t + 1)
        consume(t)
        if t < n_cached:
            refill(t)

    outs = []
    for h in range(ATT_HEADS):
        rows = slice(h * tq, (h + 1) * tq)
        denom = _rows(jnp.sum, l_ref[rows])[:, :ATT_HEAD_DIM]
        outs.append(acc_ref[rows, h * ATT_HEAD_DIM:(h + 1) * ATT_HEAD_DIM] / denom)
    o_ref[...] = jnp.concatenate(outs, axis=1)


def _sample_attention(qi, wi, qa, ki_new, ka_new, va_new, cache_idx, cache_k, cache_v, page_table, rel_bias,
                      *, tk=1024, nbis=BISECTIONS):
    s, tq, _ = qi.shape
    n_pages = page_table.shape[1]
    past = n_pages * PAGE_SIZE
    ppt = tk // PAGE_SIZE
    ntiles = past // tk + 1
    keep = min(TOPK_MAX, (past + tq) // 4)
    n_near = _n_near(tk)
    bias = _bias_tiles(rel_bias, tq, tk, n_near).reshape(n_near + 1, ATT_HEADS * tq, tk)
    utri = jnp.asarray(np.triu(np.ones((tk, tk), np.float32)), BF16)

    def per_seq(r, w):
        return pl.BlockSpec((None, r, w), lambda i, pt: (i, 0, 0))

    def full(a):
        return _resident(a.shape, lambda i, pt: (0,) * a.ndim)

    hbm = pl.BlockSpec(memory_space=pl.ANY)
    n_slots = min(SAMPLE_KV_SLOTS, past // tk)
    kern = functools.partial(_sattn_kernel, n_pages=n_pages, n_slots=n_slots, tq=tq, tk=tk, keep=keep, nbis=nbis,
                             n_near=n_near)
    grid_spec = pltpu.PrefetchScalarGridSpec(
        num_scalar_prefetch=1,
        grid=(s,),
        in_specs=[per_seq(tq, 512), per_seq(tq, IDX_HEADS), per_seq(tq, 512),
                  per_seq(IDX_DIM, tq), per_seq(ATT_WIDTH, tq), per_seq(ATT_WIDTH, tq),
                  full(bias), full(utri), hbm, hbm, hbm],
        out_specs=per_seq(tq, 512),
        scratch_shapes=[pltpu.VMEM((2, n_pages + ppt, IDX_DIM, PAGE_SIZE), F32),
                        pltpu.VMEM((n_slots, ppt, ATT_WIDTH, PAGE_SIZE), F32),
                        pltpu.VMEM((n_slots, ppt, ATT_WIDTH, PAGE_SIZE), F32),
                        pltpu.VMEM((ATT_WIDTH, tk), F32),
                        pltpu.VMEM((ATT_WIDTH, tk), F32),
                        pltpu.SemaphoreType.DMA((2 + 2 * n_slots,)),
                        pltpu.VMEM((ntiles, tq, tk), F32),
                        pltpu.VMEM((IDX_HEADS * tq, IDX_DIM), BF16),
                        pltpu.VMEM((IDX_HEADS, tq, LANES), F32),
                        pltpu.VMEM((ATT_HEADS * tq, ATT_WIDTH), BF16),
                        pltpu.VMEM((ATT_HEADS * tq, LANES), F32),
                        pltpu.VMEM((ATT_HEADS * tq, LANES), F32),
                        pltpu.VMEM((ATT_HEADS * tq, ATT_WIDTH), F32),
                        pltpu.VMEM((ATT_HEADS * tq, tk), F32),
                        pltpu.VMEM((ATT_HEADS * tq, tk), F32)])
    return pl.pallas_call(
        kern,
        grid_spec=grid_spec,
        out_shape=jax.ShapeDtypeStruct((s, tq, ATT_WIDTH), F32),
        compiler_params=pltpu.CompilerParams(dimension_semantics=("arbitrary",),
                                             vmem_limit_bytes=VMEM_LIMIT_BYTES),
        name="sattn",
    )(page_table, qi, wi, qa, ki_new, ka_new, va_new, bias, utri, cache_idx, cache_k, cache_v)


def _merge_kernel(x_ref, yr_ref, ya_ref, sgr_ref, sga_ref, wr_ref, wa_ref, wo_ref, n2_ref,
                  wfi_ref, wfo_ref, nf_ref, y_ref):
    m = (sgr_ref[...] * jnp.dot(yr_ref[...], wr_ref[...], preferred_element_type=F32)
         + sga_ref[...] * jnp.dot(ya_ref[...], wa_ref[...], preferred_element_type=F32))
    x = x_ref[...] + jnp.dot(m.astype(BF16), wo_ref[...], preferred_element_type=F32)
    h = (x * lax.rsqrt(jnp.mean(x * x, axis=-1, keepdims=True) + NORM_EPS) * n2_ref[...]).astype(BF16)
    u = jnp.dot(h, wfi_ref[:, :FFN_HIDDEN], preferred_element_type=F32)
    gt = jnp.dot(h, wfi_ref[:, FFN_HIDDEN:], preferred_element_type=F32)
    x = x + jnp.dot((jax.nn.silu(gt) * u).astype(BF16), wfo_ref[...], preferred_element_type=F32)
    y_ref[...] = x * lax.rsqrt(jnp.mean(x * x, axis=-1, keepdims=True) + NORM_EPS) * nf_ref[...]


def _merge(x, yr, ya, sgr, sga, wr, wa, wo, n2, wfi, wfo, nf, *, tm):
    rows = x.shape[0]

    def row(w):
        return pl.BlockSpec((tm, w), lambda i: (i, 0))

    def full(a):
        return _resident(a.shape, lambda i: (0,) * a.ndim)

    return pl.pallas_call(
        _merge_kernel,
        grid=(rows // tm,),
        in_specs=[row(D_MODEL), row(1024), row(ATT_WIDTH), row(D_MODEL), row(D_MODEL),
                  full(wr), full(wa), full(wo), full(n2), full(wfi), full(wfo), full(nf)],
        out_specs=row(D_MODEL),
        out_shape=jax.ShapeDtypeStruct((rows, D_MODEL), F32),
        compiler_params=pltpu.CompilerParams(dimension_semantics=("parallel",),
                                             vmem_limit_bytes=VMEM_LIMIT_BYTES),
        name="merge",
    )(x, yr, ya, sgr, sga, wr, wa, wo, n2, wfi, wfo, nf)


_W_QI_END = 5120
_W_KI_END = _W_QI_END + IDX_DIM
_W_WI_END = _W_KI_END + IDX_HEADS


def _prep_weights(w_in, idx_ln_g, idx_ln_b):
    wm = jnp.concatenate([w_in[:, :_W_QI_END], w_in[:, _W_WI_END:]], axis=1).astype(BF16)
    wk = jnp.concatenate([w_in[:, _W_QI_END:_W_KI_END]] * 2, axis=1).astype(BF16)
    ww = jnp.pad(w_in[:, _W_KI_END:_W_WI_END], ((0, 0), (0, LANES - IDX_HEADS))).astype(BF16)
    lng = jnp.concatenate([idx_ln_g] * 2)[None, :]
    lnb = jnp.concatenate([idx_ln_b] * 2)[None, :]
    return wm, wk, ww, lng, lnb


def _rot_tables(pos):
    inv_freq = ROPE_BASE ** (-jnp.arange(0, RET_QK_DIM, 2, dtype=F32) / RET_QK_DIM)
    ang = pos.astype(F32)[:, None] * inv_freq[None, :]
    cos = jnp.cos(ang)
    sin = jnp.sin(ang)
    return jnp.concatenate([cos, cos], axis=1), jnp.concatenate([-sin, sin], axis=1)


def _token_minor_cache(cache_idx, cache_k, cache_v):
    n_pool = cache_k.shape[0]
    return (jnp.transpose(cache_idx, (0, 2, 1)),
            jnp.transpose(cache_k, (0, 2, 3, 1)).reshape(n_pool, ATT_WIDTH, PAGE_SIZE),
            jnp.transpose(cache_v, (0, 2, 3, 1)).reshape(n_pool, ATT_WIDTH, PAGE_SIZE))


def _project_group(x, pos_tile, norm1_g, proj_w, *, tm, act_dtype):
    b, t, _ = x.shape
    cos_t, sin_t = _rot_tables(pos_tile)
    outs = _proj(x.reshape(b * t, D_MODEL), norm1_g[None, :], *proj_w[:3], cos_t, sin_t, *proj_w[3:],
                 tm=tm, act_dtype=act_dtype)
    names = ("qr", "kr", "vr", "gr", "qa", "ka", "va", "kab", "vab", "qi", "ki", "kk", "wi", "sgr", "sga")
    return {n: o.reshape(b, t, o.shape[-1]) for n, o in zip(names, outs)}


def _finish_group(x, p, y_ret, y_att, merge_w, *, tm):
    b, t, _ = x.shape
    flat = lambda a: a.reshape(b * t, a.shape[-1])
    y = _merge(flat(x), flat(y_ret), flat(y_att), flat(p["sgr"]), flat(p["sga"]), *merge_w, tm=tm)
    return y.reshape(b, t, D_MODEL)


def kernel(x_prompt, x_sample, cache_k, cache_v, cache_idx_k, state_ret, page_table, norm1_g, w_in,
           idx_ln_g, idx_ln_b, w_ret_branch, w_att_branch, w_out, norm2_g, w_ffn_in, w_ffn_out,
           rel_bias, norm_f_g):
    depth = w_in.shape[0]
    assert depth == 1
    l = 0
    b, t, _ = x_prompt.shape
    db, dt, _ = x_sample.shape
    past = page_table.shape[1] * PAGE_SIZE
    proj_w = _prep_weights(w_in[l], idx_ln_g[l], idx_ln_b[l])
    merge_w = (w_ret_branch[l].astype(BF16), w_att_branch[l].astype(BF16), w_out[l].astype(BF16),
               norm2_g[l][None, :], w_ffn_in[l].astype(BF16), w_ffn_out[l].astype(BF16), norm_f_g[None, :])

    tm = 256
    pp = _project_group(x_prompt, jnp.arange(t, dtype=jnp.int32), norm1_g[l], proj_w, tm=tm, act_dtype=BF16)
    s0 = jnp.zeros((b, RET_HEADS, RET_QK_DIM, RET_V_DIM), F32)
    yr_p, s_p = _retention(pp["qr"], pp["kr"], pp["vr"], pp["gr"], s0, RET_CHUNK, group=2 if b % 2 == 0 else 1)
    ya_p = _prompt_attention(pp["qi"], pp["wi"], pp["kk"], pp["qa"], pp["kab"], pp["vab"], rel_bias)
    y_prompt = _finish_group(x_prompt, pp, yr_p, ya_p, merge_w, tm=tm)

    tm_s = min(tm, db * dt)
    ps = _project_group(x_sample, past + jnp.arange(tm_s, dtype=jnp.int32) % dt, norm1_g[l], proj_w,
                        tm=tm_s, act_dtype=F32)
    yr_s, s_s = _retention(ps["qr"], ps["kr"], ps["vr"], ps["gr"], state_ret[l],
                           RET_CHUNK if dt % RET_CHUNK == 0 else dt, group=2 if db % 2 == 0 else 1)
    ya_s = _sample_attention(ps["qi"].astype(F32), ps["wi"], ps["qa"].astype(F32),
                             *(jnp.swapaxes(ps[n], 1, 2) for n in ("ki", "ka", "va")),
                             *_token_minor_cache(cache_idx_k[l], cache_k[l], cache_v[l]),
                             page_table, rel_bias)
    y_sample = _finish_group(x_sample, ps, yr_s, ya_s.astype(BF16), merge_w, tm=tm_s)

    heads = lambda a: a.reshape(a.shape[0], a.shape[1], ATT_HEADS, ATT_HEAD_DIM)[None]
    return (y_prompt, y_sample,
            heads(pp["ka"]), heads(pp["va"]), pp["ki"][None], s_p[None],
            heads(ps["ka"]), heads(ps["va"]), ps["ki"][None], s_s[None])
```

```python
import functools
import itertools
import math

import numpy as np
import jax
import jax.numpy as jnp
from jax import lax
from jax.experimental import pallas as pl
from jax.experimental.pallas import tpu as pltpu

F32 = jnp.float32
BF16 = jnp.bfloat16

D_MODEL = 1024
PAGE_SIZE = 128
RET_HEADS = 4
RET_QK_DIM = 128
RET_V_DIM = 256
RET_CHUNK = 128
ROPE_BASE = 10000.0
ATT_HEADS = 8
ATT_HEAD_DIM = 64
ATT_WIDTH = ATT_HEADS * ATT_HEAD_DIM
IDX_HEADS = 8
IDX_DIM = 64
TOPK_MAX = 256
REL_BUCKETS = 32
REL_MAX_EXACT = 16
REL_MAX_DIST = 128
REL_SATURATE = 128
REL_FIRST_SATURATED = 113
FFN_HIDDEN = 2816
NORM_EPS = 1e-6
LN_EPS = 1e-5

LANES = 128
VMEM_LIMIT_BYTES = 56 * 1024 * 1024
SAMPLE_KV_SLOTS = 6
BISECTIONS = 16

NEG_BIG = -1e30
LOG2E = math.log2(math.e)
F32_MAX = float(np.finfo(np.float32).max)

NT_DIMS = (((1,), (1,)), ((), ()))


def _resident(shape, index_map):
    return pl.BlockSpec(shape, index_map, pipeline_mode=pl.Buffered(1))


_C_QR, _C_KR, _C_VR, _C_GR = 0, 512, 1024, 2048
_C_QA, _C_KA, _C_VA, _C_QI = 3072, 3584, 4096, 4608
_C_SGR, _C_SGA, _C_END = 5120, 6144, 7168


def _proj_kernel(x_ref, g_ref, wm_ref, wk_ref, ww_ref, cos_ref, sin_ref, lng_ref, lnb_ref,
                 qr_ref, kr_ref, vr_ref, gr_ref, qa_ref, ka_ref, va_ref, kab_ref, vab_ref,
                 qi_ref, ki_ref, kk_ref, wi_ref, sgr_ref, sga_ref):
    x = x_ref[...]
    h = (x * lax.rsqrt(jnp.mean(x * x, axis=-1, keepdims=True) + NORM_EPS) * g_ref[...]).astype(BF16)

    def mm(a, b):
        return jnp.dot(h, wm_ref[:, a:b], preferred_element_type=F32)

    cos = cos_ref[...]
    sin = sin_ref[...]
    q = mm(_C_QR, _C_KR)
    k = mm(_C_KR, _C_VR)
    for hd in range(RET_HEADS):
        sl = slice(hd * RET_QK_DIM, (hd + 1) * RET_QK_DIM)
        qh = q[:, sl]
        kh = k[:, sl]
        qr_ref[:, sl] = (qh * cos + pltpu.roll(qh, RET_QK_DIM // 2, 1) * sin).astype(qr_ref.dtype)
        kr_ref[:, sl] = ((kh * cos + pltpu.roll(kh, RET_QK_DIM // 2, 1) * sin)
                         * (RET_QK_DIM ** -0.5)).astype(kr_ref.dtype)
    vr_ref[...] = mm(_C_VR, _C_GR).astype(vr_ref.dtype)
    gr_ref[...] = mm(_C_GR, _C_QA)
    qa_ref[...] = (mm(_C_QA, _C_KA) * (ATT_HEAD_DIM ** -0.5 * LOG2E)).astype(BF16)
    ka = mm(_C_KA, _C_VA)
    ka_ref[...] = ka
    kab_ref[...] = ka.astype(BF16)
    va = mm(_C_VA, _C_QI)
    va_ref[...] = va
    vab_ref[...] = va.astype(BF16)
    qi_ref[...] = (mm(_C_QI, _C_SGR) * (IDX_DIM ** -0.5)).astype(BF16)
    sgr_ref[...] = jax.nn.sigmoid(mm(_C_SGR, _C_SGA))
    sga_ref[...] = jax.nn.sigmoid(mm(_C_SGA, _C_END))

    kd = jnp.dot(h, wk_ref[...], preferred_element_type=F32)
    mu = jnp.mean(kd, axis=-1, keepdims=True)
    var = jnp.mean(jnp.square(kd - mu), axis=-1, keepdims=True)
    kn = (kd - mu) * lax.rsqrt(var + LN_EPS) * lng_ref[...] + lnb_ref[...]
    ki_ref[...] = kn[:, :IDX_DIM]
    kk_ref[...] = kn.astype(BF16)
    wi = jnp.dot(h, ww_ref[...], preferred_element_type=F32)
    wi_ref[...] = wi[:, :IDX_HEADS] * (IDX_HEADS ** -0.5)


def _proj(x, g, wm, wk, ww, cos_t, sin_t, lng, lnb, *, tm, act_dtype):
    rows = x.shape[0]
    nrot = cos_t.shape[0] // tm
    grid = (rows // tm,)

    def row(w):
        return pl.BlockSpec((tm, w), lambda i: (i, 0))

    def full(a):
        return _resident(a.shape, lambda i: (0,) * a.ndim)

    rot = pl.BlockSpec((tm, LANES), lambda i: (i % nrot, 0))
    widths = [(512, act_dtype), (512, act_dtype), (1024, act_dtype), (1024, F32),
              (512, BF16), (512, F32), (512, F32), (512, BF16), (512, BF16),
              (512, BF16), (IDX_DIM, F32), (LANES, BF16), (IDX_HEADS, F32),
              (1024, F32), (1024, F32)]
    return pl.pallas_call(
        _proj_kernel,
        grid=grid,
        in_specs=[row(D_MODEL), full(g), full(wm), full(wk), full(ww), rot, rot, full(lng), full(lnb)],
        out_specs=[row(w) for w, _ in widths],
        out_shape=[jax.ShapeDtypeStruct((rows, w), dt) for w, dt in widths],
        compiler_params=pltpu.CompilerParams(dimension_semantics=("parallel",),
                                             vmem_limit_bytes=VMEM_LIMIT_BYTES),
        name="proj",
    )(x, g, wm, wk, ww, cos_t, sin_t, lng, lnb)


def _ret_kernel(q_ref, k_ref, v_ref, g_ref, s0_ref, dm_ref, qd_ref, kd_ref, cd_ref,
                y_ref, s_ref, *, mxu_dtype):
    @pl.when(pl.program_id(1) == 0)
    def _():
        s_ref[...] = s0_ref[...]

    for seq, h in itertools.product(range(q_ref.shape[0]), range(RET_HEADS)):
        qk = slice(h * RET_QK_DIM, (h + 1) * RET_QK_DIM)
        vv = slice(h * RET_V_DIM, (h + 1) * RET_V_DIM)
        q = q_ref[seq, :, qk].astype(F32)
        k = k_ref[seq, :, qk].astype(F32)
        v = v_ref[seq, :, vv].astype(mxu_dtype)
        state = s_ref[seq, h]
        sc = lax.dot_general(q.astype(mxu_dtype), k.astype(mxu_dtype), NT_DIMS,
                             preferred_element_type=F32) * dm_ref[h]
        o = (jnp.dot(sc.astype(mxu_dtype), v, preferred_element_type=F32)
             + jnp.dot((q * qd_ref[h]).astype(mxu_dtype), state.astype(mxu_dtype),
                       preferred_element_type=F32))
        kdec = (k * kd_ref[h]).T.astype(mxu_dtype)
        s_ref[seq, h] = state * cd_ref[h] + jnp.dot(kdec, v, preferred_element_type=F32)
        mu = jnp.mean(o, axis=-1, keepdims=True)
        var = jnp.mean(jnp.square(o - mu), axis=-1, keepdims=True)
        on = (o - mu) * lax.rsqrt(var + LN_EPS)
        y_ref[seq, :, vv] = (jax.nn.silu(g_ref[seq, :, vv]) * on).astype(y_ref.dtype)


def _retention(q, k, v, g, s0, chunk, group=1):
    b, t, _ = q.shape
    n = t // chunk
    log_g = np.log1p(-np.exp2(-5.0 - np.arange(RET_HEADS, dtype=np.float64)))
    pos = np.arange(chunk, dtype=np.float64)
    diff = pos[:, None] - pos[None, :]
    dmask = np.where(diff >= 0, np.exp(log_g[:, None, None] * np.maximum(diff, 0.0)), 0.0)
    q_dec = np.broadcast_to(np.exp(log_g[:, None] * (pos + 1.0))[..., None], (RET_HEADS, chunk, RET_QK_DIM))
    k_dec = np.broadcast_to(np.exp(log_g[:, None] * (chunk - 1.0 - pos))[..., None], (RET_HEADS, chunk, RET_QK_DIM))
    c_dec = np.broadcast_to(np.exp(log_g * chunk)[:, None, None], (RET_HEADS, 1, RET_V_DIM))
    consts = [jnp.asarray(a, F32) for a in (dmask, q_dec, k_dec, c_dec)]
    mxu_dtype = BF16 if chunk % 16 == 0 else F32

    def tok(w):
        return pl.BlockSpec((group, chunk, w), lambda i, j: (i, j, 0))

    def const(a):
        return _resident(a.shape, lambda i, j: (0,) * a.ndim)

    state = pl.BlockSpec((group, RET_HEADS, RET_QK_DIM, RET_V_DIM), lambda i, j: (i, 0, 0, 0))
    return pl.pallas_call(
        functools.partial(_ret_kernel, mxu_dtype=mxu_dtype),
        grid=(b // group, n),
        in_specs=[tok(512), tok(512), tok(1024), tok(1024), state] + [const(a) for a in consts],
        out_specs=[tok(1024), state],
        out_shape=[jax.ShapeDtypeStruct((b, t, 1024), BF16),
                   jax.ShapeDtypeStruct((b, RET_HEADS, RET_QK_DIM, RET_V_DIM), F32)],
        compiler_params=pltpu.CompilerParams(dimension_semantics=("parallel", "arbitrary"),
                                             vmem_limit_bytes=VMEM_LIMIT_BYTES),
        name="retention",
    )(q, k, v, g, s0, *consts)


def _rel_bucket_np(dist):
    dist = np.asarray(dist)
    nf = np.maximum(dist, 1).astype(np.float64)
    large = REL_MAX_EXACT + (np.log(nf / REL_MAX_EXACT) / math.log(REL_MAX_DIST / REL_MAX_EXACT)
                             * (REL_BUCKETS - REL_MAX_EXACT)).astype(np.int32)
    return np.where(dist < REL_MAX_EXACT, dist, np.minimum(large, REL_BUCKETS - 1))


def _fold(op, a):
    parts = [a[:, c * LANES:(c + 1) * LANES] for c in range(a.shape[1] // LANES)]
    return functools.reduce(op, parts)


def _rows(op, a):
    return jnp.broadcast_to(op(a, axis=-1, keepdims=True), (a.shape[0], LANES))


def _rep(a, width):
    return a if width == LANES else jnp.concatenate([a] * (width // LANES), axis=1)


def _tile_loop(ntiles, body, init):
    if not isinstance(ntiles, int):
        return lax.fori_loop(0, (ntiles + 1) // 2, lambda i, c: body(2 * i + 1, body(2 * i, c)), init)
    carry = init
    for t in range(ntiles):
        carry = body(t, carry)
    return carry


def _kth_largest(tile, ntiles, lo, hi, count_lo, small, keep, tk, nbis):
    kf = float(keep)
    rows = lo.shape[0]
    zero = jnp.zeros((rows, LANES), F32)

    def count_ge(thr):
        thr_t = _rep(thr, tk)

        def body(t, c):
            return c + _fold(jnp.add, jnp.where(tile(t) >= thr_t, 1.0, 0.0))

        return _rows(jnp.sum, _tile_loop(ntiles, body, zero))

    def bisect(_, carry):
        lo, hi, c_lo = carry
        mid = 0.5 * lo + 0.5 * hi
        c_mid = count_ge(mid)
        ge = c_mid >= kf
        return jnp.where(ge, mid, lo), jnp.where(ge, hi, mid), jnp.where(ge, c_mid, c_lo)

    def quadrisect(_, carry):
        lo, hi, c_lo = carry
        cuts = [(1.0 - f) * lo + f * hi for f in (0.25, 0.5, 0.75)]
        cuts_t = [_rep(c, tk) for c in cuts]

        def body(t, counts):
            st = tile(t)
            return tuple(c + _fold(jnp.add, jnp.where(st >= ct, 1.0, 0.0)) for c, ct in zip(counts, cuts_t))

        cnt = [_rows(jnp.sum, c) for c in _tile_loop(ntiles, body, (zero, zero, zero))]
        ge = [c >= kf for c in cnt]
        new_lo = jnp.where(ge[2], cuts[2], jnp.where(ge[1], cuts[1], jnp.where(ge[0], cuts[0], lo)))
        new_hi = jnp.where(ge[2], hi, jnp.where(ge[1], cuts[2], jnp.where(ge[0], cuts[1], cuts[0])))
        new_c = jnp.where(ge[2], cnt[2], jnp.where(ge[1], cnt[1], jnp.where(ge[0], cnt[0], c_lo)))
        return new_lo, new_hi, new_c

    if isinstance(ntiles, int) and rows <= 8:
        lo, _, cge = lax.fori_loop(0, nbis // 2, quadrisect, (lo, hi, count_lo))
    else:
        lo, _, cge = lax.fori_loop(0, nbis, bisect, (lo, hi, count_lo))

    def snap_body(t, c):
        st = tile(t)
        return jnp.minimum(c, _fold(jnp.minimum, jnp.where(st >= _rep(lo, tk), st, jnp.inf)))

    val = _rows(jnp.min, _tile_loop(ntiles, snap_body, jnp.full((rows, LANES), jnp.inf, F32)))

    def above(val):
        val_t = _rep(val, tk)

        def body(t, c):
            cgt, nxt = c
            st = tile(t)
            gt = st > val_t
            cgt = cgt + _fold(jnp.add, jnp.where(gt, 1.0, 0.0))
            nxt = jnp.minimum(nxt, _fold(jnp.minimum, jnp.where(gt, st, jnp.inf)))
            return cgt, nxt

        cgt, nxt = _tile_loop(ntiles, body, (zero, jnp.full((rows, LANES), jnp.inf, F32)))
        return _rows(jnp.sum, cgt), _rows(jnp.min, nxt)

    def pending(cgt):
        return jnp.sum(jnp.where((cgt >= kf) & jnp.logical_not(small), 1.0, 0.0))

    def cond(c):
        return c[4] > 0.0

    def step(c):
        val, cge, cgt, nxt, _ = c
        advance = (cgt >= kf) & jnp.logical_not(small)
        val = jnp.where(advance, nxt, val)
        cge = jnp.where(advance, cgt, cge)
        cgt, nxt = above(val)
        return val, cge, cgt, nxt, pending(cgt)

    cgt, nxt = above(val)
    val, cge, cgt, _, _ = lax.while_loop(cond, step, (val, cge, cgt, nxt, pending(cgt)))

    tau = jnp.where(small, -F32_MAX, val)
    quota = jnp.where(small, F32_MAX, kf - cgt)
    need_fix = jnp.sum(jnp.where((cge > kf) & jnp.logical_not(small), 1.0, 0.0))
    return tau, quota, need_fix


def _drop_surplus_ties(read, write, ntiles, tau, quota, utri, tk):
    def body(t, run):
        st = read(t)
        tie = st == _rep(tau, tk)
        prefix = jnp.dot(jnp.where(tie, 1.0, 0.0).astype(BF16), utri, preferred_element_type=F32)
        over = (_rep(run, tk) + prefix) > _rep(quota, tk)
        write(t, jnp.where(tie, jnp.where(over, -jnp.inf, st), st))
        return run + _rows(jnp.sum, jnp.where(tie, 1.0, 0.0))

    lax.fori_loop(0, ntiles, body, jnp.zeros(tau.shape, F32))


def _sparse_attend(q0, idx_keys, att_keys, att_vals, qi_ref, wi_ref, qa_ref, bias_ref, utri_ref, o_ref,
                   s_ref, qim_ref, wb_ref, qam_ref, m_ref, l_ref, acc_ref, lga_ref, lgb_ref, xa_ref, xb_ref,
                   *, tq, tk, keep, nbis, n_near):
    ntiles = (q0 + tq + tk - 1) // tk
    lane = lax.broadcasted_iota(jnp.int32, (tq, LANES), 1)
    lo_half = lane < (LANES // 2)

    qi = qi_ref[...].astype(F32)
    qim_ref[...] = jnp.concatenate([qi[:, h * IDX_DIM:(h + 1) * IDX_DIM] for h in range(IDX_HEADS)],
                                   axis=0).astype(BF16)
    for h in range(IDX_HEADS):
        wb_ref[h] = jnp.broadcast_to(wi_ref[:, h:h + 1], (tq, LANES))
    for j in range(ATT_HEADS // 2):
        qp = qa_ref[:, j * LANES:(j + 1) * LANES].astype(F32)
        qam_ref[j] = jnp.concatenate([jnp.where(lo_half, qp, 0.0), jnp.where(lo_half, 0.0, qp)],
                                     axis=0).astype(BF16)

    qpos = q0 + lax.broadcasted_iota(jnp.int32, (tq, tk), 0)
    kiota = lax.broadcasted_iota(jnp.int32, (tq, tk), 1)

    def head_products(t, dst_ref):
        dst_ref[...] = lax.dot_general(qim_ref[...], idx_keys(jnp.minimum(t, ntiles - 1)), NT_DIMS,
                                       preferred_element_type=F32)

    def reduce_heads(t, x_ref, carry):
        mx, mn = carry
        tc = jnp.minimum(t, ntiles - 1)
        s = jnp.zeros((tq, tk), F32)
        for h in range(IDX_HEADS):
            s = s + _rep(wb_ref[h], tk) * jnp.maximum(x_ref[h * tq:(h + 1) * tq, :], 0.0)
        s_ref[tc] = jnp.where(tc * tk + kiota <= qpos, s, -jnp.inf)
        return jnp.maximum(mx, _fold(jnp.maximum, s)), jnp.minimum(mn, _fold(jnp.minimum, s))

    def score_two_tiles(i, carry):
        t = 2 * i
        head_products(t + 1, xb_ref)
        carry = reduce_heads(t, xa_ref, carry)
        head_products(t + 2, xa_ref)
        return reduce_heads(t + 1, xb_ref, carry)

    head_products(0, xa_ref)
    mx, mn = lax.fori_loop(0, (ntiles + 1) // 2, score_two_tiles,
                           (jnp.full((tq, LANES), -jnp.inf, F32), jnp.full((tq, LANES), jnp.inf, F32)))
    hi = _rows(jnp.max, mx)
    lo = _rows(jnp.min, mn)

    n_visible = q0 + lax.broadcasted_iota(jnp.int32, (tq, LANES), 0) + 1
    small = n_visible <= keep
    pad_tile = s_ref.shape[0] - 1
    s_ref[pad_tile] = jnp.full((tq, tk), -jnp.inf, F32)
    tau, quota, need_fix = _kth_largest(lambda t: s_ref[jnp.where(t < ntiles, t, pad_tile)], ntiles, lo, hi,
                                        n_visible.astype(F32), small, keep, tk, nbis)

    @pl.when(need_fix > 0.0)
    def _():
        def write(t, val):
            s_ref[t] = val
        _drop_surplus_ties(lambda t: s_ref[t], write, ntiles, tau, quota, utri_ref[...], tk)

    n_far = jnp.clip((q0 - LANES * n_near + tk) // tk, 0, ntiles)

    def masked_logits(t, stop, near, dst_ref):
        tc = jnp.minimum(t, ntiles - 1)
        thr = _rep(jnp.where(t < stop, tau, jnp.inf), tk)
        sel = s_ref[tc] >= thr
        for j in range(ATT_HEADS // 2):
            logits = lax.dot_general(qam_ref[j], att_keys(tc, j), NT_DIMS, preferred_element_type=F32)
            if near:
                logits = logits + bias_ref[jnp.minimum((q0 - tc * tk) // LANES, n_near), j]
            dst_ref[j] = jnp.concatenate([jnp.where(sel, logits[:tq], NEG_BIG),
                                          jnp.where(sel, logits[tq:], NEG_BIG)], axis=0)

    def sweep(start, stop, near, consume):
        masked_logits(start, stop, near, lga_ref)

        def body(i, carry):
            t = start + 2 * i
            masked_logits(t + 1, stop, near, lgb_ref)
            for j in range(ATT_HEADS // 2):
                consume(t, j, near, lga_ref[j])
            masked_logits(t + 2, stop, near, lga_ref)
            for j in range(ATT_HEADS // 2):
                consume(t + 1, j, near, lgb_ref[j])
            return carry

        lax.fori_loop(0, (stop - start + 1) // 2, body, 0)

    def take_online(t, j, near, lm):
        m_old = m_ref[j]
        row_max = _rows(jnp.max, _fold(jnp.maximum, lm))
        if near:
            m_new = jnp.maximum(m_old, row_max)
            shift = m_new
        else:
            far_bias = bias_ref[n_near, j, :, 0:LANES]
            m_new = jnp.maximum(m_old, row_max + far_bias)
            shift = m_new - far_bias
        alpha = jnp.exp2(m_old - m_new)
        p = jnp.exp2(lm - _rep(shift, tk))
        l_ref[j] = alpha * l_ref[j] + _fold(jnp.add, p)
        m_ref[j] = m_new
        acc_ref[j] = alpha * acc_ref[j] + jnp.dot(p.astype(BF16), att_vals(jnp.minimum(t, ntiles - 1), j),
                                                  preferred_element_type=F32)

    m_ref[...] = jnp.full(m_ref.shape, NEG_BIG, F32)
    l_ref[...] = jnp.zeros(l_ref.shape, F32)
    acc_ref[...] = jnp.zeros(acc_ref.shape, F32)
    sweep(0, n_far, False, take_online)
    sweep(n_far, ntiles, True, take_online)

    for j in range(ATT_HEADS // 2):
        out = acc_ref[j] / _rows(jnp.sum, l_ref[j])
        o_ref[:, j * LANES:(j + 1) * LANES] = jnp.where(lo_half, out[:tq], out[tq:]).astype(o_ref.dtype)


def _bias_tiles(rel_bias, tq, tk, n_near):
    table = rel_bias[_rel_bucket_np(np.arange(REL_SATURATE + 1))].T * LOG2E
    period = tq + tk
    m = np.arange(period)
    q_minus_k = np.where(m < tk, -m, period - m)
    tiles = []
    for v in range(n_near):
        strip = table[:, np.clip(LANES * v + q_minus_k, 0, REL_SATURATE)]
        skew = jnp.tile(strip, (1, tq))[:, :tq * (period - 1)].reshape(ATT_HEADS, tq, period - 1)
        tiles.append(skew[:, :, :tk])
    tiles.append(jnp.broadcast_to(table[:, REL_SATURATE][:, None, None], (ATT_HEADS, tq, tk)))
    return jnp.stack(tiles).astype(F32)


def _n_near(tk):
    return (tk - 1 + REL_FIRST_SATURATED - 1) // LANES + 1


def _attend_scratch(tq, tk, ntiles):
    return [pltpu.VMEM((ntiles, tq, tk), F32),
            pltpu.VMEM((IDX_HEADS * tq, IDX_DIM), BF16),
            pltpu.VMEM((IDX_HEADS, tq, LANES), F32),
            pltpu.VMEM((ATT_HEADS // 2, 2 * tq, LANES), BF16)] + [
            pltpu.VMEM((ATT_HEADS // 2, 2 * tq, LANES), F32)] * 3 + [
            pltpu.VMEM((ATT_HEADS // 2, 2 * tq, tk), F32)] * 2 + [
            pltpu.VMEM((IDX_HEADS * tq, tk), F32)] * 2


def _pattn_kernel(qi_ref, wi_ref, kk_ref, qa_ref, ka_ref, va_ref, bias_ref, utri_ref, o_ref, *scratch,
                  tq, tk, **static):
    def rows(t):
        return pl.ds(pl.multiple_of(t * tk, tk), tk)

    _sparse_attend(pl.program_id(1) * tq,
                   lambda t: kk_ref[rows(t), :IDX_DIM],
                   lambda t, j: ka_ref[rows(t), j * LANES:(j + 1) * LANES],
                   lambda t, j: va_ref[rows(t), j * LANES:(j + 1) * LANES],
                   qi_ref, wi_ref, qa_ref, bias_ref, utri_ref, o_ref, *scratch, tq=tq, tk=tk, **static)


def _prompt_attention(qi, wi, kk, qa, ka, va, rel_bias, *, tq=128, tk=512, nbis=BISECTIONS):
    b, t, _ = qi.shape
    tk = min(tk, t)
    keep = min(TOPK_MAX, max(t // 4, 1))
    n_near = _n_near(tk)
    bias = _bias_tiles(rel_bias, tq, tk, n_near).reshape(n_near + 1, ATT_HEADS // 2, 2 * tq, tk)
    utri = jnp.asarray(np.triu(np.ones((tk, tk), np.float32)), BF16)

    def qblk(w):
        return pl.BlockSpec((None, tq, w), lambda i, j: (i, j, 0))

    def seq(w):
        return _resident((None, t, w), lambda i, j: (i, 0, 0))

    def full(a):
        return _resident(a.shape, lambda i, j: (0,) * a.ndim)

    kern = functools.partial(_pattn_kernel, tq=tq, tk=tk, keep=keep, nbis=nbis, n_near=n_near)
    return pl.pallas_call(
        kern,
        grid=(b, t // tq),
        in_specs=[qblk(512), qblk(IDX_HEADS), seq(LANES), qblk(512), seq(512), seq(512), full(bias), full(utri)],
        out_specs=qblk(512),
        out_shape=jax.ShapeDtypeStruct((b, t, ATT_WIDTH), BF16),
        scratch_shapes=_attend_scratch(tq, tk, t // tk + 1),
        compiler_params=pltpu.CompilerParams(dimension_semantics=("parallel", "arbitrary"),
                                             vmem_limit_bytes=VMEM_LIMIT_BYTES),
        name="pattn",
    )(qi, wi, kk, qa, ka, va, bias, utri)


def _sattn_kernel(pt_ref, qi_ref, wi_ref, qa_ref, kin_ref, kan_ref, van_ref, bias_ref, utri_ref,
                  cidx_hbm, ck_hbm, cv_hbm, o_ref,
                  idx_buf, k_buf, v_buf, knew_buf, vnew_buf, sem,
                  s_ref, qim_ref, wb_ref, qbd_ref, m_ref, l_ref, acc_ref, lga_ref, lgb_ref,
                  *, n_pages, n_slots, tq, tk, keep, nbis, n_near):
    seq = pl.program_id(0)
    n_seq = pl.num_programs(0)
    past = n_pages * PAGE_SIZE
    ppt = tk // PAGE_SIZE
    n_cached = past // tk
    ntiles = n_cached + 1
    ibuf = seq % 2

    def idx_copy(s, p):
        return pltpu.make_async_copy(cidx_hbm.at[pt_ref[s, p]], idx_buf.at[s % 2, p], sem.at[s % 2])

    def tile_copies(s, t):
        slot = (s * n_cached + t) % n_slots
        copies = []
        for i in range(ppt):
            page = pt_ref[s, t * ppt + i]
            copies.append(pltpu.make_async_copy(ck_hbm.at[page], k_buf.at[slot, i], sem.at[2 + slot]))
            copies.append(pltpu.make_async_copy(cv_hbm.at[page], v_buf.at[slot, i], sem.at[2 + n_slots + slot]))
        return copies

    def start_idx(s):
        def body(p, carry):
            idx_copy(s, p).start()
            return carry
        lax.fori_loop(0, n_pages, body, 0)

    def start_tile(s, t):
        for c in tile_copies(s, t):
            c.start()

    @pl.when(seq == 0)
    def _():
        start_idx(0)
        for t in range(n_slots):
            start_tile(0, t)
        idx_buf[:, n_pages:n_pages + ppt] = jnp.zeros((2, ppt, IDX_DIM, PAGE_SIZE), F32)
        knew_buf[...] = jnp.zeros(knew_buf.shape, F32)
        vnew_buf[...] = jnp.zeros(vnew_buf.shape, F32)

    idx_buf[ibuf, n_pages, :, 0:tq] = kin_ref[...]
    knew_buf[:, 0:tq] = kan_ref[...]
    vnew_buf[:, 0:tq] = van_ref[...]

    qi = qi_ref[...]
    qim_ref[...] = jnp.concatenate([qi[:, h * IDX_DIM:(h + 1) * IDX_DIM] for h in range(IDX_HEADS)],
                                   axis=0).astype(BF16)
    for h in range(IDX_HEADS):
        wb_ref[h] = jnp.broadcast_to(wi_ref[:, h:h + 1], (tq, LANES))
    head_of_lane = lax.broadcasted_iota(jnp.int32, (tq, ATT_WIDTH), 1) // ATT_HEAD_DIM
    qa = qa_ref[...]
    qbd_ref[...] = jnp.concatenate([jnp.where(head_of_lane == h, qa, 0.0) for h in range(ATT_HEADS)],
                                   axis=0).astype(BF16)

    def wait_idx(p, carry):
        idx_copy(seq, p).wait()
        return carry

    lax.fori_loop(0, n_pages, wait_idx, 0)

    @pl.when(seq + 1 < n_seq)
    def _():
        start_idx(seq + 1)

    def lanes_of(pages):
        return jnp.concatenate([pages[i] for i in range(ppt)], axis=1)

    lg_refs = (lga_ref, lgb_ref)

    def head_products(t):
        keys_t = lanes_of(idx_buf[ibuf, t * ppt:(t + 1) * ppt])
        lg_refs[t % 2][...] = jnp.dot(qim_ref[...], keys_t.astype(BF16), preferred_element_type=F32)

    new_visible = (lax.broadcasted_iota(jnp.int32, (tq, tk), 1)
                   <= lax.broadcasted_iota(jnp.int32, (tq, tk), 0))
    mx = jnp.full((tq, LANES), -jnp.inf, F32)
    mn = jnp.full((tq, LANES), jnp.inf, F32)
    head_products(0)
    for t in range(ntiles):
        if t + 1 < ntiles:
            head_products(t + 1)
        s = jnp.zeros((tq, tk), F32)
        for h in range(IDX_HEADS):
            s = s + _rep(wb_ref[h], tk) * jnp.maximum(lg_refs[t % 2][h * tq:(h + 1) * tq, :], 0.0)
        s_ref[t] = jnp.where(new_visible, s, -jnp.inf) if t == n_cached else s
        mx = jnp.maximum(mx, _fold(jnp.maximum, s))
        mn = jnp.minimum(mn, _fold(jnp.minimum, s))
    hi = _rows(jnp.max, mx)
    lo = _rows(jnp.min, mn)

    small = jnp.zeros((tq, LANES), jnp.bool_)
    n_visible = (past + 1 + lax.broadcasted_iota(jnp.int32, (tq, LANES), 0)).astype(F32)
    tau, quota, need_fix = _kth_largest(lambda t: s_ref[t], ntiles, lo, hi, n_visible, small, keep, tk, nbis)

    @pl.when(need_fix > 0.0)
    def _():
        def write(t, val):
            s_ref[t] = val
        _drop_surplus_ties(lambda t: s_ref[t], write, ntiles, tau, quota, utri_ref[...], tk)

    m_ref[...] = jnp.full(m_ref.shape, NEG_BIG, F32)
    l_ref[...] = jnp.zeros(l_ref.shape, F32)
    acc_ref[...] = jnp.zeros(acc_ref.shape, F32)
    tau_t = _rep(tau, tk)

    variants = [min((past - t * tk) // LANES, n_near) for t in range(ntiles)]

    def slot_of(t):
        return (seq * n_cached + t) % n_slots

    def masked_logits(t):
        if t < n_cached:
            for c in tile_copies(seq, t):
                c.wait()
            keys_t = lanes_of(k_buf[slot_of(t)])
        else:
            keys_t = knew_buf[...]
        hide = jnp.where(s_ref[t] >= tau_t, 0.0, NEG_BIG)
        lm = (jnp.dot(qbd_ref[...], keys_t.astype(BF16), preferred_element_type=F32)
              + jnp.concatenate([hide] * ATT_HEADS, axis=0))
        if variants[t] < n_near:
            lm = lm + bias_ref[variants[t]]
        lg_refs[t % 2][...] = lm

    def consume(t):
        lm = lg_refs[t % 2][...]
        m_old = m_ref[...]
        row_max = _rows(jnp.max, _fold(jnp.maximum, lm))
        if variants[t] < n_near:
            m_new = jnp.maximum(m_old, row_max)
            shift = m_new
        else:
            far_bias = bias_ref[n_near, :, 0:LANES]
            m_new = jnp.maximum(m_old, row_max + far_bias)
            shift = m_new - far_bias
        alpha = jnp.exp2(m_old - m_new)
        p = jnp.exp2(lm - _rep(shift, tk))
        l_ref[...] = alpha * l_ref[...] + _fold(jnp.add, p)
        m_ref[...] = m_new
        vals_t = lanes_of(v_buf[slot_of(t)]) if t < n_cached else vnew_buf[...]
        pv = lax.dot_general(p.astype(BF16), vals_t.astype(BF16), NT_DIMS, preferred_element_type=F32)
        acc_ref[...] = _rep(alpha, ATT_WIDTH) * acc_ref[...] + pv

    def refill(t):
        ahead = t + n_slots
        nxt_seq = seq + ahead // n_cached

        @pl.when(nxt_seq < n_seq)
        def _():
            start_tile(nxt_seq, ahead % n_cached)

    masked_logits(0)
    for t in range(ntiles):
        if t + 1 < ntiles:
            masked_logits(t + 1)
        consume(t)
        if t < n_cached:
            refill(t)

    outs = []
    for h in range(ATT_HEADS):
        rows = slice(h * tq, (h + 1) * tq)
        denom = _rows(jnp.sum, l_ref[rows])[:, :ATT_HEAD_DIM]
        outs.append(acc_ref[rows, h * ATT_HEAD_DIM:(h + 1) * ATT_HEAD_DIM] / denom)
    o_ref[...] = jnp.concatenate(outs, axis=1)


def _sample_attention(qi, wi, qa, ki_new, ka_new, va_new, cache_idx, cache_k, cache_v, page_table, rel_bias,
                      *, tk=1024, nbis=BISECTIONS):
    s, tq, _ = qi.shape
    n_pages = page_table.shape[1]
    past = n_pages * PAGE_SIZE
    ppt = tk // PAGE_SIZE
    ntiles = past // tk + 1
    keep = min(TOPK_MAX, (past + tq) // 4)
    n_near = _n_near(tk)
    bias = _bias_tiles(rel_bias, tq, tk, n_near).reshape(n_near + 1, ATT_HEADS * tq, tk)
    utri = jnp.asarray(np.triu(np.ones((tk, tk), np.float32)), BF16)

    def per_seq(r, w):
        return pl.BlockSpec((None, r, w), lambda i, pt: (i, 0, 0))

    def full(a):
        return _resident(a.shape, lambda i, pt: (0,) * a.ndim)

    hbm = pl.BlockSpec(memory_space=pl.ANY)
    n_slots = min(SAMPLE_KV_SLOTS, past // tk)
    kern = functools.partial(_sattn_kernel, n_pages=n_pages, n_slots=n_slots, tq=tq, tk=tk, keep=keep, nbis=nbis,
                             n_near=n_near)
    grid_spec = pltpu.PrefetchScalarGridSpec(
        num_scalar_prefetch=1,
        grid=(s,),
        in_specs=[per_seq(tq, 512), per_seq(tq, IDX_HEADS), per_seq(tq, 512),
                  per_seq(IDX_DIM, tq), per_seq(ATT_WIDTH, tq), per_seq(ATT_WIDTH, tq),
                  full(bias), full(utri), hbm, hbm, hbm],
        out_specs=per_seq(tq, 512),
        scratch_shapes=[pltpu.VMEM((2, n_pages + ppt, IDX_DIM, PAGE_SIZE), F32),
                        pltpu.VMEM((n_slots, ppt, ATT_WIDTH, PAGE_SIZE), F32),
                        pltpu.VMEM((n_slots, ppt, ATT_WIDTH, PAGE_SIZE), F32),
                        pltpu.VMEM((ATT_WIDTH, tk), F32),
                        pltpu.VMEM((ATT_WIDTH, tk), F32),
                        pltpu.SemaphoreType.DMA((2 + 2 * n_slots,)),
                        pltpu.VMEM((ntiles, tq, tk), F32),
                        pltpu.VMEM((IDX_HEADS * tq, IDX_DIM), BF16),
                        pltpu.VMEM((IDX_HEADS, tq, LANES), F32),
                        pltpu.VMEM((ATT_HEADS * tq, ATT_WIDTH), BF16),
                        pltpu.VMEM((ATT_HEADS * tq, LANES), F32),
                        pltpu.VMEM((ATT_HEADS * tq, LANES), F32),
                        pltpu.VMEM((ATT_HEADS * tq, ATT_WIDTH), F32),
                        pltpu.VMEM((ATT_HEADS * tq, tk), F32),
                        pltpu.VMEM((ATT_HEADS * tq, tk), F32)])
    return pl.pallas_call(
        kern,
        grid_spec=grid_spec,
        out_shape=jax.ShapeDtypeStruct((s, tq, ATT_WIDTH), F32),
        compiler_params=pltpu.CompilerParams(dimension_semantics=("arbitrary",),
                                             vmem_limit_bytes=VMEM_LIMIT_BYTES),
        name="sattn",
    )(page_table, qi, wi, qa, ki_new, ka_new, va_new, bias, utri, cache_idx, cache_k, cache_v)


def _merge_kernel(x_ref, yr_ref, ya_ref, sgr_ref, sga_ref, wr_ref, wa_ref, wo_ref, n2_ref,
                  wfi_ref, wfo_ref, nf_ref, y_ref):
    m = (sgr_ref[...] * jnp.dot(yr_ref[...], wr_ref[...], preferred_element_type=F32)
         + sga_ref[...] * jnp.dot(ya_ref[...], wa_ref[...], preferred_element_type=F32))
    x = x_ref[...] + jnp.dot(m.astype(BF16), wo_ref[...], preferred_element_type=F32)
    h = (x * lax.rsqrt(jnp.mean(x * x, axis=-1, keepdims=True) + NORM_EPS) * n2_ref[...]).astype(BF16)
    u = jnp.dot(h, wfi_ref[:, :FFN_HIDDEN], preferred_element_type=F32)
    gt = jnp.dot(h, wfi_ref[:, FFN_HIDDEN:], preferred_element_type=F32)
    x = x + jnp.dot((jax.nn.silu(gt) * u).astype(BF16), wfo_ref[...], preferred_element_type=F32)
    y_ref[...] = x * lax.rsqrt(jnp.mean(x * x, axis=-1, keepdims=True) + NORM_EPS) * nf_ref[...]


def _merge(x, yr, ya, sgr, sga, wr, wa, wo, n2, wfi, wfo, nf, *, tm):
    rows = x.shape[0]

    def row(w):
        return pl.BlockSpec((tm, w), lambda i: (i, 0))

    def full(a):
        return _resident(a.shape, lambda i: (0,) * a.ndim)

    return pl.pallas_call(
        _merge_kernel,
        grid=(rows // tm,),
        in_specs=[row(D_MODEL), row(1024), row(ATT_WIDTH), row(D_MODEL), row(D_MODEL),
                  full(wr), full(wa), full(wo), full(n2), full(wfi), full(wfo), full(nf)],
        out_specs=row(D_MODEL),
        out_shape=jax.ShapeDtypeStruct((rows, D_MODEL), F32),
        compiler_params=pltpu.CompilerParams(dimension_semantics=("parallel",),
                                             vmem_limit_bytes=VMEM_LIMIT_BYTES),
        name="merge",
    )(x, yr, ya, sgr, sga, wr, wa, wo, n2, wfi, wfo, nf)


_W_QI_END = 5120
_W_KI_END = _W_QI_END + IDX_DIM
_W_WI_END = _W_KI_END + IDX_HEADS


def _prep_weights(w_in, idx_ln_g, idx_ln_b):
    wm = jnp.concatenate([w_in[:, :_W_QI_END], w_in[:, _W_WI_END:]], axis=1).astype(BF16)
    wk = jnp.concatenate([w_in[:, _W_QI_END:_W_KI_END]] * 2, axis=1).astype(BF16)
    ww = jnp.pad(w_in[:, _W_KI_END:_W_WI_END], ((0, 0), (0, LANES - IDX_HEADS))).astype(BF16)
    lng = jnp.concatenate([idx_ln_g] * 2)[None, :]
    lnb = jnp.concatenate([idx_ln_b] * 2)[None, :]
    return wm, wk, ww, lng, lnb


def _rot_tables(pos):
    inv_freq = ROPE_BASE ** (-jnp.arange(0, RET_QK_DIM, 2, dtype=F32) / RET_QK_DIM)
    ang = pos.astype(F32)[:, None] * inv_freq[None, :]
    cos = jnp.cos(ang)
    sin = jnp.sin(ang)
    return jnp.concatenate([cos, cos], axis=1), jnp.concatenate([-sin, sin], axis=1)


def _token_minor_cache(cache_idx, cache_k, cache_v):
    n_pool = cache_k.shape[0]
    return (jnp.transpose(cache_idx, (0, 2, 1)),
            jnp.transpose(cache_k, (0, 2, 3, 1)).reshape(n_pool, ATT_WIDTH, PAGE_SIZE),
            jnp.transpose(cache_v, (0, 2, 3, 1)).reshape(n_pool, ATT_WIDTH, PAGE_SIZE))


def _project_group(x, pos_tile, norm1_g, proj_w, *, tm, act_dtype):
    b, t, _ = x.shape
    cos_t, sin_t = _rot_tables(pos_tile)
    outs = _proj(x.reshape(b * t, D_MODEL), norm1_g[None, :], *proj_w[:3], cos_t, sin_t, *proj_w[3:],
                 tm=tm, act_dtype=act_dtype)
    names = ("qr", "kr", "vr", "gr", "qa", "ka", "va", "kab", "vab", "qi", "ki", "kk", "wi", "sgr", "sga")
    return {n: o.reshape(b, t, o.shape[-1]) for n, o in zip(names, outs)}


def _finish_group(x, p, y_ret, y_att, merge_w, *, tm):
    b, t, _ = x.shape
    flat = lambda a: a.reshape(b * t, a.shape[-1])
    y = _merge(flat(x), flat(y_ret), flat(y_att), flat(p["sgr"]), flat(p["sga"]), *merge_w, tm=tm)
    return y.reshape(b, t, D_MODEL)


def kernel(x_prompt, x_sample, cache_k, cache_v, cache_idx_k, state_ret, page_table, norm1_g, w_in,
           idx_ln_g, idx_ln_b, w_ret_branch, w_att_branch, w_out, norm2_g, w_ffn_in, w_ffn_out,
           rel_bias, norm_f_g):
    depth = w_in.shape[0]
    assert depth == 1
    l = 0
    b, t, _ = x_prompt.shape
    db, dt, _ = x_sample.shape
    past = page_table.shape[1] * PAGE_SIZE
    proj_w = _prep_weights(w_in[l], idx_ln_g[l], idx_ln_b[l])
    merge_w = (w_ret_branch[l].astype(BF16), w_att_branch[l].astype(BF16), w_out[l].astype(BF16),
               norm2_g[l][None, :], w_ffn_in[l].astype(BF16), w_ffn_out[l].astype(BF16), norm_f_g[None, :])

    tm = 256
    pp = _project_group(x_prompt, jnp.arange(t, dtype=jnp.int32), norm1_g[l], proj_w, tm=tm, act_dtype=BF16)
    s0 = jnp.zeros((b, RET_HEADS, RET_QK_DIM, RET_V_DIM), F32)
    yr_p, s_p = _retention(pp["qr"], pp["kr"], pp["vr"], pp["gr"], s0, RET_CHUNK, group=2 if b % 2 == 0 else 1)
    ya_p = _prompt_attention(pp["qi"], pp["wi"], pp["kk"], pp["qa"], pp["kab"], pp["vab"], rel_bias)
    y_prompt = _finish_group(x_prompt, pp, yr_p, ya_p, merge_w, tm=tm)

    tm_s = min(tm, db * dt)
    ps = _project_group(x_sample, past + jnp.arange(tm_s, dtype=jnp.int32) % dt, norm1_g[l], proj_w,
                        tm=tm_s, act_dtype=F32)
    yr_s, s_s = _retention(ps["qr"], ps["kr"], ps["vr"], ps["gr"], state_ret[l],
                           RET_CHUNK if dt % RET_CHUNK == 0 else dt, group=2 if db % 2 == 0 else 1)
    ya_s = _sample_attention(ps["qi"].astype(F32), ps["wi"], ps["qa"].astype(F32),
                             *(jnp.swapaxes(ps[n], 1, 2) for n in ("ki", "ka", "va")),
                             *_token_minor_cache(cache_idx_k[l], cache_k[l], cache_v[l]),
                             page_table, rel_bias)
    y_sample = _finish_group(x_sample, ps, yr_s, ya_s.astype(BF16), merge_w, tm=tm_s)

    heads = lambda a: a.reshape(a.shape[0], a.shape[1], ATT_HEADS, ATT_HEAD_DIM)[None]
    return (y_prompt, y_sample,
            heads(pp["ka"]), heads(pp["va"]), pp["ki"][None], s_p[None],
            heads(ps["ka"]), heads(ps["va"]), ps["ki"][None], s_s[None])
```
